```python
import jax, jax.numpy as jnp
from jax import lax
import numpy as np

D_MODEL = 1024
BATCH = 8
SEQ = 2048
DEPTH = 4
DEC_BATCH = 128
DEC_SEQ = 8
PAST_LEN = 16384
PAGE_SIZE = 128

D_MIX = D_MODEL
D_POOL = D_MIX // 2
D_CONV = D_MIX - D_POOL
POOL_WINDOWS = (2, 4, 8, 16)
N_POOL_GROUPS = len(POOL_WINDOWS)
POOL_GROUP = D_POOL // N_POOL_GROUPS
POOL_HIST = max(POOL_WINDOWS) - 1
CONV_WIDTH = 31
CONV_HIST = CONV_WIDTH - 1
D_FF = ((8 * D_MODEL // 3 + 127) // 128) * 128
FFN_CONV_WIDTH = 3
FFN_HIST = FFN_CONV_WIDTH - 1
D_PLE = 256
RMS_EPS = 1e-6
LN_EPS = 1e-5

kernel_name = "hybrid_pool_conformer_decoder_step"


def rmsnorm(x, g):
    xf = x.astype(jnp.float32)
    y = xf * lax.rsqrt(jnp.mean(xf * xf, axis=-1, keepdims=True) + RMS_EPS)
    return (y * g.astype(jnp.float32)).astype(x.dtype)


def layernorm(x, g, b):
    xf = x.astype(jnp.float32)
    mu = jnp.mean(xf, axis=-1, keepdims=True)
    xc = xf - mu
    var = jnp.mean(xc * xc, axis=-1, keepdims=True)
    y = xc * lax.rsqrt(var + LN_EPS) * g.astype(jnp.float32) + b.astype(jnp.float32)
    return y.astype(x.dtype)


def causal_depthwise(ext, w, b):
    c = ext.shape[-1]
    out = lax.conv_general_dilated(ext, w[:, None, :].astype(ext.dtype), window_strides=(1,),
                                   padding='VALID', dimension_numbers=('NWC', 'WIO', 'NWC'),
                                   feature_group_count=c)
    return out + b.astype(ext.dtype)


def pool_mix(a_ext, pos0, w_pool, pool_scale):
    bsz, ext_len, _ = a_ext.shape
    t_new = ext_len - POOL_HIST
    af = a_ext.astype(jnp.float32)
    csum = jnp.concatenate([jnp.zeros_like(af[:, :1]), lax.cumsum(af, axis=1)], axis=1)
    end = csum[:, POOL_HIST + 1:]
    xt = af[:, POOL_HIST:]
    pos = pos0 + jnp.arange(t_new, dtype=jnp.int32)
    outs = []
    for g, w in enumerate(POOL_WINDOWS):
        lo, hi = g * POOL_GROUP, (g + 1) * POOL_GROUP
        start = csum[:, POOL_HIST + 1 - w: POOL_HIST + 1 - w + t_new, lo:hi]
        cnt = jnp.minimum(pos + 1, w).astype(jnp.float32)[None, :, None]
        outs.append((end[..., lo:hi] - start) / cnt - xt[..., lo:hi])
    z = jnp.stack(outs, axis=2)
    y = jnp.einsum('btgc,gcd->btgd', z, w_pool.astype(jnp.float32)).reshape(bsz, t_new, D_POOL)
    return (y * pool_scale.astype(jnp.float32)).astype(a_ext.dtype)


def trunk_layer(x, p, st_pool, st_conv, st_ffn, pos0,
                g_mix, w_in, w_pool, pool_scale, w_dw, b_dw, ln_g, ln_b, w_pw,
                g_out_a, g_out_b, w_out, g_ffn, w_ffn_in, w_ffn_dw, b_ffn_dw, w_ffn_out,
                w_ple, g_ple, w_ple_gate):
    n = rmsnorm(x, g_mix)
    z = n @ w_in
    a = z[..., :D_POOL]
    u = z[..., D_POOL:]
    a_ext = jnp.concatenate([st_pool.astype(a.dtype), a], axis=1)
    ya = pool_mix(a_ext, pos0, w_pool, pool_scale)
    new_pool = a_ext[:, -POOL_HIST:]
    glu = u[..., :D_CONV] * jax.nn.sigmoid(u[..., D_CONV:])
    c_ext = jnp.concatenate([st_conv.astype(glu.dtype), glu], axis=1)
    c = layernorm(causal_depthwise(c_ext, w_dw, b_dw), ln_g, ln_b)
    yb = jax.nn.silu(c) @ w_pw
    new_conv = c_ext[:, -CONV_HIST:]
    mix = jnp.concatenate([rmsnorm(ya, g_out_a), rmsnorm(yb, g_out_b)], axis=-1) @ w_out
    x = x + mix
    n2 = rmsnorm(x, g_ffn)
    gu = n2 @ w_ffn_in
    gate = gu[..., :D_FF]
    up = gu[..., D_FF:]
    g_ext = jnp.concatenate([st_ffn.astype(gate.dtype), gate], axis=1)
    gc = causal_depthwise(g_ext, w_ffn_dw, b_ffn_dw)
    x = x + (jax.nn.silu(gc) * up) @ w_ffn_out
    new_ffn = g_ext[:, -FFN_HIST:]
    e = rmsnorm(p.astype(x.dtype) @ w_ple, g_ple)
    x = x + jax.nn.sigmoid(x @ w_ple_gate) * e
    return x, new_pool, new_conv, new_ffn


def run_trunk(x, p, st_pool, st_conv, st_ffn, pos0, weights, final_norm):
    pools, convs, ffns = [], [], []
    for i in range(DEPTH):
        lw = [w[i] for w in weights]
        x, npool, nconv, nffn = trunk_layer(x, p[i], st_pool[i], st_conv[i], st_ffn[i], pos0, *lw)
        pools.append(npool)
        convs.append(nconv)
        ffns.append(nffn)
    return rmsnorm(x, final_norm), jnp.stack(pools), jnp.stack(convs), jnp.stack(ffns)


def setup_inputs(seed: int = 0) -> dict:
    key = jax.random.key(seed)
    ks = jax.random.split(key, 32)
    f32 = jnp.float32
    nrm = lambda k, shape, s: jax.random.normal(k, shape, f32) * s
    gain = lambda k, shape: 1.0 + 0.05 * jax.random.normal(k, shape, f32)
    return {
        "x_prompt": nrm(ks[0], (BATCH, SEQ, D_MODEL), 1.0),
        "x_sample": nrm(ks[1], (DEC_BATCH, DEC_SEQ, D_MODEL), 1.0),
        "state_pool": nrm(ks[2], (DEPTH, DEC_BATCH, POOL_HIST, D_POOL), 1.0),
        "state_conv": nrm(ks[3], (DEPTH, DEC_BATCH, CONV_HIST, D_CONV), 0.5),
        "state_ffn": nrm(ks[4], (DEPTH, DEC_BATCH, FFN_HIST, D_FF), 1.0),
        "p_prompt": nrm(ks[5], (DEPTH, BATCH, SEQ, D_PLE), 1.0),
        "p_sample": nrm(ks[6], (DEPTH, DEC_BATCH, DEC_SEQ, D_PLE), 1.0),
        "g_mix": gain(ks[7], (DEPTH, D_MODEL)),
        "w_in": nrm(ks[8], (DEPTH, D_MODEL, D_POOL + 2 * D_CONV), D_MODEL ** -0.5),
        "w_pool": nrm(ks[9], (DEPTH, N_POOL_GROUPS, POOL_GROUP, POOL_GROUP), POOL_GROUP ** -0.5),
        "pool_scale": 1.0 + 0.1 * jax.random.normal(ks[10], (DEPTH, D_POOL), f32),
        "w_dw": nrm(ks[11], (DEPTH, CONV_WIDTH, D_CONV), CONV_WIDTH ** -0.5),
        "b_dw": nrm(ks[12], (DEPTH, D_CONV), 0.01),
        "ln_g": gain(ks[13], (DEPTH, D_CONV)),
        "ln_b": nrm(ks[14], (DEPTH, D_CONV), 0.01),
        "w_pw": nrm(ks[15], (DEPTH, D_CONV, D_CONV), D_CONV ** -0.5),
        "g_out_a": gain(ks[16], (DEPTH, D_POOL)),
        "g_out_b": gain(ks[17], (DEPTH, D_CONV)),
        "w_out": nrm(ks[18], (DEPTH, D_MIX, D_MODEL), D_MIX ** -0.5),
        "g_ffn": gain(ks[19], (DEPTH, D_MODEL)),
        "w_ffn_in": nrm(ks[20], (DEPTH, D_MODEL, 2 * D_FF), D_MODEL ** -0.5),
        "w_ffn_dw": nrm(ks[21], (DEPTH, FFN_CONV_WIDTH, D_FF), FFN_CONV_WIDTH ** -0.5),
        "b_ffn_dw": nrm(ks[22], (DEPTH, D_FF), 0.01),
        "w_ffn_out": nrm(ks[23], (DEPTH, D_FF, D_MODEL), D_FF ** -0.5),
        "w_ple": nrm(ks[24], (DEPTH, D_PLE, D_MODEL), D_PLE ** -0.5),
        "g_ple": gain(ks[25], (DEPTH, D_MODEL)),
        "w_ple_gate": nrm(ks[26], (DEPTH, D_MODEL, D_MODEL), D_MODEL ** -0.5),
        "final_norm": gain(ks[27], (D_MODEL,)),
    }


def reference(x_prompt, x_sample, state_pool, state_conv, state_ffn, p_prompt, p_sample,
              g_mix, w_in, w_pool, pool_scale, w_dw, b_dw, ln_g, ln_b, w_pw,
              g_out_a, g_out_b, w_out, g_ffn, w_ffn_in, w_ffn_dw, b_ffn_dw, w_ffn_out,
              w_ple, g_ple, w_ple_gate, final_norm):
    weights = (g_mix, w_in, w_pool, pool_scale, w_dw, b_dw, ln_g, ln_b, w_pw,
               g_out_a, g_out_b, w_out, g_ffn, w_ffn_in, w_ffn_dw, b_ffn_dw, w_ffn_out,
               w_ple, g_ple, w_ple_gate)
    dt = x_prompt.dtype
    zp_pool = jnp.zeros((DEPTH, BATCH, POOL_HIST, D_POOL), dt)
    zp_conv = jnp.zeros((DEPTH, BATCH, CONV_HIST, D_CONV), dt)
    zp_ffn = jnp.zeros((DEPTH, BATCH, FFN_HIST, D_FF), dt)
    y_prompt, pool_p, conv_p, ffn_p = run_trunk(x_prompt, p_prompt, zp_pool, zp_conv, zp_ffn, 0,
                                                weights, final_norm)
    y_sample, pool_s, conv_s, ffn_s = run_trunk(x_sample, p_sample, state_pool, state_conv, state_ffn,
                                                PAST_LEN, weights, final_norm)
    return (y_prompt, y_sample, pool_p, conv_p, ffn_p, pool_s, conv_s, ffn_s)
```

```python
import functools

import jax
import jax.numpy as jnp
from jax import lax
from jax.experimental import pallas as pl
from jax.experimental.pallas import tpu as pltpu

D_MODEL = 1024
D_POOL = 512
D_CONV = 512
POOL_WINDOWS = (2, 4, 8, 16)
POOL_GROUP = 128
POOL_HIST = 15
CONV_WIDTH = 31
CONV_HIST = 30
D_FF = 2816
FFN_HIST = 2
D_PLE = 256
RMS_EPS = 1e-6
LN_EPS = 1e-5
PAST_LEN = 16384

SUBLANES = 8
VMEM_LIMIT_BYTES = 56 * 1024 * 1024

POOL_PAD = 16
CONV_PAD = 32
FFN_PAD = 8
FFN_CHUNKS = ((0, 512), (512, 512), (1024, 512), (1536, 512), (2048, 512), (2560, 256))
FFN_CHUNK_MAX = 512

BF16 = jnp.bfloat16
F32 = jnp.float32


def _dot(a, b):
    return jnp.dot(a, b, preferred_element_type=F32)


def _rms(x, g):
    return x * lax.rsqrt(jnp.mean(x * x, axis=-1, keepdims=True) + RMS_EPS) * g


def _layernorm(x, g, b):
    mu = jnp.mean(x, axis=-1, keepdims=True)
    xc = x - mu
    var = jnp.mean(xc * xc, axis=-1, keepdims=True)
    return xc * lax.rsqrt(var + LN_EPS) * g + b


def _silu(x):
    return x * jax.nn.sigmoid(x)


def _mixer_in(x, w):
    n = _rms(x, w["g_mix"][...]).astype(BF16)
    a = _dot(n, w["w_in"][:, 0:D_POOL])
    u1 = _dot(n, w["w_in"][:, D_POOL:D_POOL + D_CONV])
    u2 = _dot(n, w["w_in"][:, D_POOL + D_CONV:D_POOL + 2 * D_CONV])
    return a, u1 * jax.nn.sigmoid(u2)


def _pool_project(zs, w):
    ys = [_dot(z.astype(BF16), w["w_pool"][g]) for g, z in enumerate(zs)]
    return jnp.concatenate(ys, axis=1) * w["pool_scale"][...]


def _conv_post(c, w):
    return _silu(_layernorm(c, w["ln_g"][...], w["ln_b"][...])).astype(BF16)


def _merge(x, ya, cs, w):
    yb = _dot(cs, w["w_pw"][...])
    m = jnp.concatenate([_rms(ya, w["g_out_a"][...]), _rms(yb, w["g_out_b"][...])], axis=1)
    return x + _dot(m.astype(BF16), w["w_out"][...])


def _ffn(x, w, conv_gate, keep_gate):
    n2 = _rms(x, w["g_ffn"][...]).astype(BF16)
    acc = x
    for c0, cf in FFN_CHUNKS:
        gate = _dot(n2, w["w_ffn_in"][:, c0:c0 + cf])
        up = _dot(n2, w["w_ffn_in"][:, D_FF + c0:D_FF + c0 + cf])
        gc = conv_gate(gate, c0, cf)
        keep_gate(gate, c0, cf)
        h = (_silu(gc) * up).astype(BF16)
        acc = acc + _dot(h, w["w_ffn_out"][c0:c0 + cf, :])
    return acc


def _ple(x, p, w):
    e = _rms(_dot(p.astype(BF16), w["w_ple"][...]), w["g_ple"][...])
    return x + jax.nn.sigmoid(_dot(x.astype(BF16), w["w_ple_gate"][...])) * e


WEIGHT_NAMES = ("g_mix", "w_in", "w_pool", "pool_scale", "w_dw", "b_dw", "ln_g", "ln_b", "w_pw",
                "g_out_a", "g_out_b", "w_out", "g_ffn", "w_ffn_in", "w_ffn_dw", "b_ffn_dw",
                "w_ffn_out", "w_ple", "g_ple", "w_ple_gate", "final_norm")
N_W = len(WEIGHT_NAMES)


def _prompt_layer_kernel(*refs, tm, conv_rows, final):
    x_ref, p_ref = refs[0], refs[1]
    w = dict(zip(WEIGHT_NAMES, refs[2:2 + N_W]))
    y_ref, npool_ref, nconv_ref, nffn_ref = refs[2 + N_W:6 + N_W]
    a_ext, c_ext, cs_ref, g_ext, ffn_hist = refs[6 + N_W:]
    t = pl.program_id(1)

    @pl.when(t == 0)
    def _():
        a_ext[0:POOL_PAD, :] = jnp.zeros((POOL_PAD, D_POOL), F32)
        c_ext[0:CONV_PAD, :] = jnp.zeros((CONV_PAD, D_CONV), F32)
        ffn_hist[...] = jnp.zeros((FFN_PAD, D_FF), F32)

    x = x_ref[...]
    a, glu = _mixer_in(x, w)

    a_ext[POOL_PAD:POOL_PAD + tm, :] = a
    pos = t * tm + lax.broadcasted_iota(jnp.int32, (tm, 1), 0)
    zs = []
    for g, win in enumerate(POOL_WINDOWS):
        lo = g * POOL_GROUP
        tok = a_ext[POOL_PAD:POOL_PAD + tm, lo:lo + POOL_GROUP]
        s = tok
        for j in range(1, win):
            s = s + a_ext[POOL_PAD - j:POOL_PAD - j + tm, lo:lo + POOL_GROUP]
        cnt = jnp.minimum(pos + 1, win).astype(F32)
        zs.append(s / cnt - tok)
    ya = _pool_project(zs, w)
    npool_ref[...] = a_ext[tm + POOL_PAD - POOL_HIST:tm + POOL_PAD, :]
    a_ext[0:POOL_PAD, :] = a_ext[tm:tm + POOL_PAD, :]

    c_ext[CONV_PAD:CONV_PAD + tm, :] = glu
    base = CONV_PAD - CONV_HIST
    for r0 in range(0, tm, conv_rows):
        acc = jnp.broadcast_to(w["b_dw"][...], (conv_rows, D_CONV))
        for k in range(CONV_WIDTH):
            acc = acc + w["w_dw"][k:k + 1, :] * c_ext[base + r0 + k:base + r0 + k + conv_rows, :]
        cs_ref[r0:r0 + conv_rows, :] = _conv_post(acc, w)
    nconv_ref[...] = c_ext[tm + CONV_PAD - CONV_HIST:tm + CONV_PAD, :]
    c_ext[0:CONV_PAD, :] = c_ext[tm:tm + CONV_PAD, :]

    x = _merge(x, ya, cs_ref[...], w)

    def conv_gate(gate, c0, cf):
        g_ext[0:FFN_PAD, 0:cf] = ffn_hist[:, c0:c0 + cf]
        g_ext[FFN_PAD:FFN_PAD + tm, 0:cf] = gate
        wd = w["w_ffn_dw"]
        return (wd[0:1, c0:c0 + cf] * g_ext[FFN_PAD - 2:FFN_PAD - 2 + tm, 0:cf]
                + wd[1:2, c0:c0 + cf] * g_ext[FFN_PAD - 1:FFN_PAD - 1 + tm, 0:cf]
                + wd[2:3, c0:c0 + cf] * gate + w["b_ffn_dw"][:, c0:c0 + cf])

    def keep_gate(gate, c0, cf):
        ffn_hist[:, c0:c0 + cf] = gate[tm - FFN_PAD:tm, :]
        nffn_ref[:, c0:c0 + cf] = gate[tm - FFN_HIST:tm, :]

    x = _ffn(x, w, conv_gate, keep_gate)
    x = _ple(x, p_ref[...], w)
    if final:
        x = _rms(x, w["final_norm"][...])
    y_ref[...] = x


def _prompt_layer(x, p, weights, layer, final, tm=512, conv_rows=64):
    bsz, seq, _ = x.shape
    nd = {3: lambda b, t: (layer, 0, 0), 4: lambda b, t: (layer, 0, 0, 0)}
    w_specs = []
    for name in WEIGHT_NAMES:
        arr = weights[name]
        if name == "final_norm":
            w_specs.append(pl.BlockSpec(arr.shape, lambda b, t: (0, 0), pipeline_mode=pl.Buffered(1)))
        else:
            w_specs.append(pl.BlockSpec((None,) + arr.shape[1:], nd[arr.ndim],
                                        pipeline_mode=pl.Buffered(1)))
    in_specs = [
        pl.BlockSpec((None, tm, D_MODEL), lambda b, t: (b, t, 0)),
        pl.BlockSpec((None, None, tm, D_PLE), lambda b, t: (layer, b, t, 0)),
    ] + w_specs
    out_shape = (
        jax.ShapeDtypeStruct((bsz, seq, D_MODEL), F32),
        jax.ShapeDtypeStruct((bsz, POOL_HIST, D_POOL), F32),
        jax.ShapeDtypeStruct((bsz, CONV_HIST, D_CONV), F32),
        jax.ShapeDtypeStruct((bsz, FFN_HIST, D_FF), F32),
    )
    out_specs = (
        pl.BlockSpec((None, tm, D_MODEL), lambda b, t: (b, t, 0)),
        pl.BlockSpec((None, POOL_HIST, D_POOL), lambda b, t: (b, 0, 0)),
        pl.BlockSpec((None, CONV_HIST, D_CONV), lambda b, t: (b, 0, 0)),
        pl.BlockSpec((None, FFN_HIST, D_FF), lambda b, t: (b, 0, 0)),
    )
    scratch = [
        pltpu.VMEM((POOL_PAD + tm, D_POOL), F32),
        pltpu.VMEM((CONV_PAD + tm, D_CONV), F32),
        pltpu.VMEM((tm, D_CONV), BF16),
        pltpu.VMEM((FFN_PAD + tm, FFN_CHUNK_MAX), F32),
        pltpu.VMEM((FFN_PAD, D_FF), F32),
    ]
    return pl.pallas_call(
        functools.partial(_prompt_layer_kernel, tm=tm, conv_rows=conv_rows, final=final),
        grid=(bsz, seq // tm),
        in_specs=in_specs,
        out_specs=out_specs,
        out_shape=out_shape,
        scratch_shapes=scratch,
        compiler_params=pltpu.CompilerParams(
            dimension_semantics=("arbitrary", "arbitrary"),
            vmem_limit_bytes=VMEM_LIMIT_BYTES),
        name=f"prompt_layer{layer}",
    )(x, p, *[weights[n] for n in WEIGHT_NAMES])


def _sample_kernel(*refs, sb, steps, depth, conv_rows):
    x_ref, p_ref, sp_ref, sc_ref, sf_ref = refs[0:5]
    w = dict(zip(WEIGHT_NAMES, refs[5:5 + N_W]))
    y_ref, npool_ref, nconv_ref, nffn_ref = refs[5 + N_W:9 + N_W]
    (cs_ref,) = refs[9 + N_W:]
    layer = pl.program_id(0)
    seqs = pl.ds(pl.multiple_of(pl.program_id(1) * sb, sb), sb)
    m = sb * steps

    @pl.when(layer == 0)
    def _():
        y_ref[seqs, :] = x_ref[...]

    x = jnp.concatenate([y_ref[seqs, t * D_MODEL:(t + 1) * D_MODEL] for t in range(steps)], axis=0)
    a, glu = _mixer_in(x, w)

    def pool_ext(i, lo, hi):
        if i < POOL_HIST:
            return sp_ref[:, i * D_POOL + lo:i * D_POOL + hi]
        return a[(i - POOL_HIST) * sb:(i - POOL_HIST + 1) * sb, lo:hi]

    zs = []
    for g, win in enumerate(POOL_WINDOWS):
        lo = g * POOL_GROUP
        rows = []
        for t in range(steps):
            i = POOL_HIST + t
            tok = pool_ext(i, lo, lo + POOL_GROUP)
            s = tok
            for j in range(1, win):
                s = s + pool_ext(i - j, lo, lo + POOL_GROUP)
            cnt = float(min(PAST_LEN + t + 1, win))
            rows.append(s / cnt - tok)
        zs.append(jnp.concatenate(rows, axis=0))
    ya = _pool_project(zs, w)
    for h in range(POOL_HIST):
        npool_ref[:, h * D_POOL:(h + 1) * D_POOL] = pool_ext(h + steps, 0, D_POOL)

    def conv_ext(i, r0):
        if i < CONV_HIST:
            return sc_ref[r0:r0 + conv_rows, i * D_CONV:(i + 1) * D_CONV]
        return glu[(i - CONV_HIST) * sb + r0:(i - CONV_HIST) * sb + r0 + conv_rows, :]

    for t in range(steps):
        for r0 in range(0, sb, conv_rows):
            acc = jnp.broadcast_to(w["b_dw"][...], (conv_rows, D_CONV))
            for k in range(CONV_WIDTH):
                acc = acc + w["w_dw"][k:k + 1, :] * conv_ext(t + k, r0)
            cs_ref[t * sb + r0:t * sb + r0 + conv_rows, :] = _conv_post(acc, w)
    for h in range(CONV_HIST):
        i = h + steps
        if i < CONV_HIST:
            nconv_ref[:, h * D_CONV:(h + 1) * D_CONV] = sc_ref[:, i * D_CONV:(i + 1) * D_CONV]
        else:
            nconv_ref[:, h * D_CONV:(h + 1) * D_CONV] = glu[(i - CONV_HIST) * sb:(i - CONV_HIST + 1) * sb, :]

    x = _merge(x, ya, cs_ref[...], w)

    def conv_gate(gate, c0, cf):
        h0 = sf_ref[:, c0:c0 + cf]
        h1 = sf_ref[:, D_FF + c0:D_FF + c0 + cf]
        prev1 = jnp.concatenate([h1, gate[0:m - sb, :]], axis=0)
        prev2 = jnp.concatenate([h0, h1, gate[0:m - 2 * sb, :]], axis=0)
        wd = w["w_ffn_dw"]
        return (wd[0:1, c0:c0 + cf] * prev2 + wd[1:2, c0:c0 + cf] * prev1
                + wd[2:3, c0:c0 + cf] * gate + w["b_ffn_dw"][:, c0:c0 + cf])

    def keep_gate(gate, c0, cf):
        for h in range(FFN_HIST):
            tt = steps - FFN_HIST + h
            nffn_ref[:, h * D_FF + c0:h * D_FF + c0 + cf] = gate[tt * sb:(tt + 1) * sb, :]

    x = _ffn(x, w, conv_gate, keep_gate)
    p = jnp.concatenate([p_ref[:, t * D_PLE:(t + 1) * D_PLE] for t in range(steps)], axis=0)
    x = _ple(x, p, w)

    @pl.when(layer < depth - 1)
    def _():
        for t in range(steps):
            y_ref[seqs, t * D_MODEL:(t + 1) * D_MODEL] = x[t * sb:(t + 1) * sb, :]

    @pl.when(layer == depth - 1)
    def _():
        y = _rms(x, w["final_norm"][...])
        for t in range(steps):
            y_ref[seqs, t * D_MODEL:(t + 1) * D_MODEL] = y[t * sb:(t + 1) * sb, :]


def _sample_trunk(x, p, st_pool, st_conv, st_ffn, weights, sb=32, conv_rows=32):
    nseq, steps, _ = x.shape
    depth = p.shape[0]
    nblk = nseq // sb
    x2 = x.reshape(nseq, steps * D_MODEL)
    p2 = p.reshape(depth, nseq, steps * D_PLE)
    sp = st_pool.reshape(depth, nseq, POOL_HIST * D_POOL)
    sc = st_conv.reshape(depth, nseq, CONV_HIST * D_CONV)
    sf = st_ffn.reshape(depth, nseq, FFN_HIST * D_FF)

    nd = {3: lambda l, s: (l, 0, 0), 4: lambda l, s: (l, 0, 0, 0)}
    w_specs = []
    for name in WEIGHT_NAMES:
        arr = weights[name]
        if name == "final_norm":
            w_specs.append(pl.BlockSpec(arr.shape, lambda l, s: (0, 0), pipeline_mode=pl.Buffered(1)))
        else:
            w_specs.append(pl.BlockSpec((None,) + arr.shape[1:], nd[arr.ndim],
                                        pipeline_mode=pl.Buffered(1)))

    def state_spec(width):
        return pl.BlockSpec((None, sb, width), lambda l, s: (l, s, 0))

    in_specs = [
        pl.BlockSpec((sb, steps * D_MODEL), lambda l, s: (s, 0)),
        state_spec(steps * D_PLE),
        state_spec(POOL_HIST * D_POOL), state_spec(CONV_HIST * D_CONV), state_spec(FFN_HIST * D_FF),
    ] + w_specs
    out_shape = (
        jax.ShapeDtypeStruct((nseq, steps * D_MODEL), F32),
        jax.ShapeDtypeStruct(sp.shape, F32),
        jax.ShapeDtypeStruct(sc.shape, F32),
        jax.ShapeDtypeStruct(sf.shape, F32),
    )
    out_specs = (
        pl.BlockSpec((nseq, steps * D_MODEL), lambda l, s: (0, 0)),
        state_spec(POOL_HIST * D_POOL), state_spec(CONV_HIST * D_CONV), state_spec(FFN_HIST * D_FF),
    )
    scratch = [
        pltpu.VMEM((sb * steps, D_CONV), BF16),
    ]
    y, npool, nconv, nffn = pl.pallas_call(
        functools.partial(_sample_kernel, sb=sb, steps=steps, depth=depth, conv_rows=conv_rows),
        grid=(depth, nblk),
        in_specs=in_specs,
        out_specs=out_specs,
        out_shape=out_shape,
        scratch_shapes=scratch,
        compiler_params=pltpu.CompilerParams(
            dimension_semantics=("arbitrary", "arbitrary"),
            vmem_limit_bytes=VMEM_LIMIT_BYTES),
        name="sample_trunk",
    )(x2, p2, sp, sc, sf, *[weights[n] for n in WEIGHT_NAMES])
    return (y.reshape(nseq, steps, D_MODEL),
            npool.reshape(depth, nseq, POOL_HIST, D_POOL),
            nconv.reshape(depth, nseq, CONV_HIST, D_CONV),
            nffn.reshape(depth, nseq, FFN_HIST, D_FF))


def kernel(x_prompt, x_sample, state_pool, state_conv, state_ffn, p_prompt, p_sample, g_mix, w_in, w_pool, pool_scale, w_dw, b_dw, ln_g, ln_b, w_pw, g_out_a, g_out_b, w_out, g_ffn, w_ffn_in, w_ffn_dw, b_ffn_dw, w_ffn_out, w_ple, g_ple, w_ple_gate, final_norm):
    depth = g_mix.shape[0]
    row = lambda v: v.reshape(depth, 1, v.shape[-1])
    weights = {
        "g_mix": row(g_mix), "w_in": w_in.astype(BF16), "w_pool": w_pool.astype(BF16),
        "pool_scale": row(pool_scale), "w_dw": w_dw, "b_dw": row(b_dw), "ln_g": row(ln_g),
        "ln_b": row(ln_b), "w_pw": w_pw.astype(BF16), "g_out_a": row(g_out_a),
        "g_out_b": row(g_out_b), "w_out": w_out.astype(BF16), "g_ffn": row(g_ffn),
        "w_ffn_in": w_ffn_in.astype(BF16), "w_ffn_dw": w_ffn_dw, "b_ffn_dw": row(b_ffn_dw),
        "w_ffn_out": w_ffn_out.astype(BF16), "w_ple": w_ple.astype(BF16), "g_ple": row(g_ple),
        "w_ple_gate": w_ple_gate.astype(BF16), "final_norm": final_norm.reshape(1, D_MODEL),
    }

    y_s, pool_s, conv_s, ffn_s = _sample_trunk(x_sample, p_sample, state_pool, state_conv,
                                               state_ffn, weights)

    x = x_prompt
    pools, convs, ffns = [], [], []
    for layer in range(depth):
        x, npool, nconv, nffn = _prompt_layer(x, p_prompt, weights, layer, final=layer == depth - 1)
        pools.append(npool)
        convs.append(nconv)
        ffns.append(nffn)
    return (x, y_s, jnp.stack(pools), jnp.stack(convs), jnp.stack(ffns), pool_s, conv_s, ffn_s)
```

```python
import functools

import jax
import jax.numpy as jnp
from jax import lax
from jax.experimental import pallas as pl
from jax.experimental.pallas import tpu as pltpu

D_MODEL = 1024
D_POOL = 512
D_CONV = 512
POOL_WINDOWS = (2, 4, 8, 16)
POOL_GROUP = 128
POOL_HIST = 15
CONV_WIDTH = 31
CONV_HIST = 30
D_FF = 2816
FFN_HIST = 2
D_PLE = 256
RMS_EPS = 1e-6
LN_EPS = 1e-5
PAST_LEN = 16384

SUBLANES = 8
VMEM_LIMIT_BYTES = 56 * 1024 * 1024

POOL_PAD = 16
CONV_PAD = 32
FFN_CHUNKS = ((0, 512), (512, 512), (1024, 512), (1536, 512), (2048, 512), (2560, 256))
FFN_CHUNK_MAX = 512

BF16 = jnp.bfloat16
F32 = jnp.float32


def _dot(a, b):
    return jnp.dot(a, b, preferred_element_type=F32)


def _rms(x, g):
    return x * lax.rsqrt(jnp.mean(x * x, axis=-1, keepdims=True) + RMS_EPS) * g


def _layernorm(x, g, b):
    mu = jnp.mean(x, axis=-1, keepdims=True)
    xc = x - mu
    var = jnp.mean(xc * xc, axis=-1, keepdims=True)
    return xc * lax.rsqrt(var + LN_EPS) * g + b


def _silu(x):
    return x * jax.nn.sigmoid(x)


def _mixer_in(x, w):
    n = _rms(x, w["g_mix"][...]).astype(BF16)
    a = _dot(n, w["w_in"][:, 0:D_POOL])
    u1 = _dot(n, w["w_in"][:, D_POOL:D_POOL + D_CONV])
    u2 = _dot(n, w["w_in"][:, D_POOL + D_CONV:D_POOL + 2 * D_CONV])
    return a, u1 * jax.nn.sigmoid(u2)


def _pool_project(zs, w):
    ys = [_dot(z.astype(BF16), w["w_pool"][g]) for g, z in enumerate(zs)]
    return jnp.concatenate(ys, axis=1) * w["pool_scale"][...]


def _conv_post(c, w):
    return _silu(_layernorm(c, w["ln_g"][...], w["ln_b"][...])).astype(BF16)


def _merge(x, ya, cs, w):
    yb = _dot(cs, w["w_pw"][...])
    m = jnp.concatenate([_rms(ya, w["g_out_a"][...]), _rms(yb, w["g_out_b"][...])], axis=1)
    return x + _dot(m.astype(BF16), w["w_out"][...])


def _ffn(x, w, conv_gate, keep_gate):
    n2 = _rms(x, w["g_ffn"][...]).astype(BF16)
    acc = x
    for c0, cf in FFN_CHUNKS:
        gate = _dot(n2, w["w_ffn_in"][:, c0:c0 + cf])
        up = _dot(n2, w["w_ffn_in"][:, D_FF + c0:D_FF + c0 + cf])
        gc = conv_gate(gate, c0, cf)
        keep_gate(gate, c0, cf)
        h = (_silu(gc) * up).astype(BF16)
        acc = acc + _dot(h, w["w_ffn_out"][c0:c0 + cf, :])
    return acc


def _ple(x, p, w):
    e = _rms(_dot(p.astype(BF16), w["w_ple"][...]), w["g_ple"][...])
    return x + jax.nn.sigmoid(_dot(x.astype(BF16), w["w_ple_gate"][...])) * e


WEIGHT_NAMES = ("g_mix", "w_in", "w_pool", "pool_scale", "w_dw", "b_dw", "ln_g", "ln_b", "w_pw",
                "g_out_a", "g_out_b", "w_out", "g_ffn", "w_ffn_in", "w_ffn_dw", "b_ffn_dw",
                "w_ffn_out", "w_ple", "g_ple", "w_ple_gate", "final_norm")
N_W = len(WEIGHT_NAMES)


def _weight_specs(weights, ngrid):
    def spec(arr, pick_layer):
        rest = (0,) * (arr.ndim - 1)
        if pick_layer is None:
            return pl.BlockSpec(arr.shape, lambda *g: (0,) + rest, pipeline_mode=pl.Buffered(1))
        return pl.BlockSpec((None,) + arr.shape[1:], lambda *g: (pick_layer(*g),) + rest,
                            pipeline_mode=pl.Buffered(1))
    return lambda pick_layer: [
        spec(weights[n], None if n == "final_norm" else pick_layer) for n in WEIGHT_NAMES]


def _prompt_layer_kernel(*refs, tt, nb, conv_rows, final):
    x_ref, p_ref = refs[0], refs[1]
    w = dict(zip(WEIGHT_NAMES, refs[2:2 + N_W]))
    y_ref, npool_ref, nconv_ref, nffn_ref = refs[2 + N_W:6 + N_W]
    a_ext, c_ext, cs_ref, g_ext, ffn_hist = refs[6 + N_W:]
    step = pl.program_id(0)
    tm = tt * nb
    pool0 = POOL_PAD * nb
    conv0 = CONV_PAD * nb
    ffn0 = FFN_HIST * nb

    @pl.when(step == 0)
    def _():
        a_ext[0:pool0, :] = jnp.zeros((pool0, D_POOL), F32)
        c_ext[0:conv0, :] = jnp.zeros((conv0, D_CONV), F32)
        ffn_hist[...] = jnp.zeros((ffn0, D_FF), F32)

    x = x_ref[...]
    a, glu = _mixer_in(x, w)

    a_ext[pool0:pool0 + tm, :] = a
    pos = step * tt + lax.broadcasted_iota(jnp.int32, (tm, 1), 0) // nb
    zs = []
    for g, win in enumerate(POOL_WINDOWS):
        lo = g * POOL_GROUP
        s = a_ext[pool0 - (win - 1) * nb:pool0 + tm, lo:lo + POOL_GROUP]
        span = 1
        while span < win:
            s = s[span * nb:, :] + s[:-span * nb, :]
            span *= 2
        inv_cnt = 1.0 / jnp.minimum(pos + 1, win).astype(F32)
        zs.append(s * inv_cnt - a[:, lo:lo + POOL_GROUP])
    ya = _pool_project(zs, w)
    npool_ref[...] = a_ext[tm + pool0 - POOL_HIST * nb:tm + pool0, :]
    a_ext[0:pool0, :] = a_ext[tm:tm + pool0, :]

    c_ext[conv0:conv0 + tm, :] = glu
    base = conv0 - CONV_HIST * nb
    for r0 in range(0, tm, conv_rows):
        acc = jnp.broadcast_to(w["b_dw"][...], (conv_rows, D_CONV))
        for k in range(CONV_WIDTH):
            acc = acc + w["w_dw"][k:k + 1, :] * c_ext[base + r0 + k * nb:base + r0 + k * nb + conv_rows, :]
        cs_ref[r0:r0 + conv_rows, :] = _conv_post(acc, w)
    nconv_ref[...] = c_ext[tm + conv0 - CONV_HIST * nb:tm + conv0, :]
    c_ext[0:conv0, :] = c_ext[tm:tm + conv0, :]

    x = _merge(x, ya, cs_ref[...], w)

    def conv_gate(gate, c0, cf):
        g_ext[0:ffn0, 0:cf] = ffn_hist[:, c0:c0 + cf]
        g_ext[ffn0:ffn0 + tm, 0:cf] = gate
        wd = w["w_ffn_dw"]
        return (wd[0:1, c0:c0 + cf] * g_ext[0:tm, 0:cf]
                + wd[1:2, c0:c0 + cf] * g_ext[nb:nb + tm, 0:cf]
                + wd[2:3, c0:c0 + cf] * gate + w["b_ffn_dw"][:, c0:c0 + cf])

    def keep_gate(gate, c0, cf):
        ffn_hist[:, c0:c0 + cf] = gate[tm - ffn0:tm, :]
        nffn_ref[:, c0:c0 + cf] = gate[tm - ffn0:tm, :]

    x = _ffn(x, w, conv_gate, keep_gate)
    x = _ple(x, p_ref[...], w)
    if final:
        x = _rms(x, w["final_norm"][...])
    y_ref[...] = x


def _prompt_layer(x, p, weights, layer, final, nb, tt=64, conv_rows=32):
    rows = x.shape[0]
    assert nb % SUBLANES == 0 and rows % (tt * nb) == 0
    tm = tt * nb
    in_specs = [
        pl.BlockSpec((tm, D_MODEL), lambda t: (t, 0)),
        pl.BlockSpec((None, tm, D_PLE), lambda t: (layer, t, 0)),
    ] + _weight_specs(weights, 1)(lambda t: layer)
    out_shape = (
        jax.ShapeDtypeStruct((rows, D_MODEL), F32),
        jax.ShapeDtypeStruct((POOL_HIST * nb, D_POOL), F32),
        jax.ShapeDtypeStruct((CONV_HIST * nb, D_CONV), F32),
        jax.ShapeDtypeStruct((FFN_HIST * nb, D_FF), F32),
    )
    out_specs = (
        pl.BlockSpec((tm, D_MODEL), lambda t: (t, 0)),
        pl.BlockSpec((POOL_HIST * nb, D_POOL), lambda t: (0, 0)),
        pl.BlockSpec((CONV_HIST * nb, D_CONV), lambda t: (0, 0)),
        pl.BlockSpec((FFN_HIST * nb, D_FF), lambda t: (0, 0)),
    )
    scratch = [
        pltpu.VMEM((POOL_PAD * nb + tm, D_POOL), F32),
        pltpu.VMEM((CONV_PAD * nb + tm, D_CONV), F32),
        pltpu.VMEM((tm, D_CONV), BF16),
        pltpu.VMEM((FFN_HIST * nb + tm, FFN_CHUNK_MAX), F32),
        pltpu.VMEM((FFN_HIST * nb, D_FF), F32),
    ]
    return pl.pallas_call(
        functools.partial(_prompt_layer_kernel, tt=tt, nb=nb, conv_rows=conv_rows, final=final),
        grid=(rows // tm,),
        in_specs=in_specs,
        out_specs=out_specs,
        out_shape=out_shape,
        scratch_shapes=scratch,
        compiler_params=pltpu.CompilerParams(
            dimension_semantics=("arbitrary",),
            vmem_limit_bytes=VMEM_LIMIT_BYTES),
        name=f"prompt_layer{layer}",
    )(x, p, *[weights[n] for n in WEIGHT_NAMES])


def _sample_kernel(*refs, sb, steps, depth, conv_rows):
    x_ref, p_ref, sp_ref, sc_ref, sf_ref = refs[0:5]
    w = dict(zip(WEIGHT_NAMES, refs[5:5 + N_W]))
    y_ref, npool_ref, nconv_ref, nffn_ref = refs[5 + N_W:9 + N_W]
    (cs_ref,) = refs[9 + N_W:]
    layer = pl.program_id(0)
    seqs = pl.ds(pl.multiple_of(pl.program_id(1) * sb, sb), sb)
    m = sb * steps

    @pl.when(layer == 0)
    def _():
        y_ref[seqs, :] = x_ref[...]

    x = jnp.concatenate([y_ref[seqs, t * D_MODEL:(t + 1) * D_MODEL] for t in range(steps)], axis=0)
    a, glu = _mixer_in(x, w)

    def pool_ext(i, lo, hi):
        if i < POOL_HIST:
            return sp_ref[:, i * D_POOL + lo:i * D_POOL + hi]
        return a[(i - POOL_HIST) * sb:(i - POOL_HIST + 1) * sb, lo:hi]

    zs = []
    for g, win in enumerate(POOL_WINDOWS):
        lo = g * POOL_GROUP
        rows = []
        for t in range(steps):
            i = POOL_HIST + t
            tok = pool_ext(i, lo, lo + POOL_GROUP)
            s = tok
            for j in range(1, win):
                s = s + pool_ext(i - j, lo, lo + POOL_GROUP)
            cnt = float(min(PAST_LEN + t + 1, win))
            rows.append(s / cnt - tok)
        zs.append(jnp.concatenate(rows, axis=0))
    ya = _pool_project(zs, w)
    for h in range(POOL_HIST):
        npool_ref[:, h * D_POOL:(h + 1) * D_POOL] = pool_ext(h + steps, 0, D_POOL)

    def conv_ext(i, r0):
        if i < CONV_HIST:
            return sc_ref[r0:r0 + conv_rows, i * D_CONV:(i + 1) * D_CONV]
        return glu[(i - CONV_HIST) * sb + r0:(i - CONV_HIST) * sb + r0 + conv_rows, :]

    for t in range(steps):
        for r0 in range(0, sb, conv_rows):
            acc = jnp.broadcast_to(w["b_dw"][...], (conv_rows, D_CONV))
            for k in range(CONV_WIDTH):
                acc = acc + w["w_dw"][k:k + 1, :] * conv_ext(t + k, r0)
            cs_ref[t * sb + r0:t * sb + r0 + conv_rows, :] = _conv_post(acc, w)
    for h in range(CONV_HIST):
        i = h + steps
        if i < CONV_HIST:
            nconv_ref[:, h * D_CONV:(h + 1) * D_CONV] = sc_ref[:, i * D_CONV:(i + 1) * D_CONV]
        else:
            nconv_ref[:, h * D_CONV:(h + 1) * D_CONV] = glu[(i - CONV_HIST) * sb:(i - CONV_HIST + 1) * sb, :]

    x = _merge(x, ya, cs_ref[...], w)

    def conv_gate(gate, c0, cf):
        h0 = sf_ref[:, c0:c0 + cf]
        h1 = sf_ref[:, D_FF + c0:D_FF + c0 + cf]
        prev1 = jnp.concatenate([h1, gate[0:m - sb, :]], axis=0)
        prev2 = jnp.concatenate([h0, h1, gate[0:m - 2 * sb, :]], axis=0)
        wd = w["w_ffn_dw"]
        return (wd[0:1, c0:c0 + cf] * prev2 + wd[1:2, c0:c0 + cf] * prev1
                + wd[2:3, c0:c0 + cf] * gate + w["b_ffn_dw"][:, c0:c0 + cf])

    def keep_gate(gate, c0, cf):
        for h in range(FFN_HIST):
            tt = steps - FFN_HIST + h
            nffn_ref[:, h * D_FF + c0:h * D_FF + c0 + cf] = gate[tt * sb:(tt + 1) * sb, :]

    x = _ffn(x, w, conv_gate, keep_gate)
    p = jnp.concatenate([p_ref[:, t * D_PLE:(t + 1) * D_PLE] for t in range(steps)], axis=0)
    x = _ple(x, p, w)

    @pl.when(layer < depth - 1)
    def _():
        for t in range(steps):
            y_ref[seqs, t * D_MODEL:(t + 1) * D_MODEL] = x[t * sb:(t + 1) * sb, :]

    @pl.when(layer == depth - 1)
    def _():
        y = _rms(x, w["final_norm"][...])
        for t in range(steps):
            y_ref[seqs, t * D_MODEL:(t + 1) * D_MODEL] = y[t * sb:(t + 1) * sb, :]


def _sample_trunk(x, p, st_pool, st_conv, st_ffn, weights, sb=32, conv_rows=32):
    nseq, steps, _ = x.shape
    depth = p.shape[0]
    nblk = nseq // sb
    x2 = x.reshape(nseq, steps * D_MODEL)
    p2 = p.reshape(depth, nseq, steps * D_PLE)
    sp = st_pool.reshape(depth, nseq, POOL_HIST * D_POOL)
    sc = st_conv.reshape(depth, nseq, CONV_HIST * D_CONV)
    sf = st_ffn.reshape(depth, nseq, FFN_HIST * D_FF)

    def state_spec(width):
        return pl.BlockSpec((None, sb, width), lambda l, s: (l, s, 0))

    in_specs = [
        pl.BlockSpec((sb, steps * D_MODEL), lambda l, s: (s, 0)),
        state_spec(steps * D_PLE),
        state_spec(POOL_HIST * D_POOL), state_spec(CONV_HIST * D_CONV), state_spec(FFN_HIST * D_FF),
    ] + _weight_specs(weights, 2)(lambda l, s: l)
    out_shape = (
        jax.ShapeDtypeStruct((nseq, steps * D_MODEL), F32),
        jax.ShapeDtypeStruct(sp.shape, F32),
        jax.ShapeDtypeStruct(sc.shape, F32),
        jax.ShapeDtypeStruct(sf.shape, F32),
    )
    out_specs = (
        pl.BlockSpec((nseq, steps * D_MODEL), lambda l, s: (0, 0)),
        state_spec(POOL_HIST * D_POOL), state_spec(CONV_HIST * D_CONV), state_spec(FFN_HIST * D_FF),
    )
    scratch = [
        pltpu.VMEM((sb * steps, D_CONV), BF16),
    ]
    y, npool, nconv, nffn = pl.pallas_call(
        functools.partial(_sample_kernel, sb=sb, steps=steps, depth=depth, conv_rows=conv_rows),
        grid=(depth, nblk),
        in_specs=in_specs,
        out_specs=out_specs,
        out_shape=out_shape,
        scratch_shapes=scratch,
        compiler_params=pltpu.CompilerParams(
            dimension_semantics=("arbitrary", "arbitrary"),
            vmem_limit_bytes=VMEM_LIMIT_BYTES),
        name="sample_trunk",
    )(x2, p2, sp, sc, sf, *[weights[n] for n in WEIGHT_NAMES])
    return (y.reshape(nseq, steps, D_MODEL),
            npool.reshape(depth, nseq, POOL_HIST, D_POOL),
            nconv.reshape(depth, nseq, CONV_HIST, D_CONV),
            nffn.reshape(depth, nseq, FFN_HIST, D_FF))


def kernel(x_prompt, x_sample, state_pool, state_conv, state_ffn, p_prompt, p_sample, g_mix, w_in, w_pool, pool_scale, w_dw, b_dw, ln_g, ln_b, w_pw, g_out_a, g_out_b, w_out, g_ffn, w_ffn_in, w_ffn_dw, b_ffn_dw, w_ffn_out, w_ple, g_ple, w_ple_gate, final_norm):
    depth = g_mix.shape[0]
    row = lambda v: v.reshape(depth, 1, v.shape[-1])
    weights = {
        "g_mix": row(g_mix), "w_in": w_in.astype(BF16), "w_pool": w_pool.astype(BF16),
        "pool_scale": row(pool_scale), "w_dw": w_dw, "b_dw": row(b_dw), "ln_g": row(ln_g),
        "ln_b": row(ln_b), "w_pw": w_pw.astype(BF16), "g_out_a": row(g_out_a),
        "g_out_b": row(g_out_b), "w_out": w_out.astype(BF16), "g_ffn": row(g_ffn),
        "w_ffn_in": w_ffn_in.astype(BF16), "w_ffn_dw": w_ffn_dw, "b_ffn_dw": row(b_ffn_dw),
        "w_ffn_out": w_ffn_out.astype(BF16), "w_ple": w_ple.astype(BF16), "g_ple": row(g_ple),
        "w_ple_gate": w_ple_gate.astype(BF16), "final_norm": final_norm.reshape(1, D_MODEL),
    }

    y_s, pool_s, conv_s, ffn_s = _sample_trunk(x_sample, p_sample, state_pool, state_conv,
                                               state_ffn, weights)

    nb, seq, _ = x_prompt.shape
    x = jnp.swapaxes(x_prompt, 0, 1).reshape(seq * nb, D_MODEL)
    p = jnp.swapaxes(p_prompt, 1, 2).reshape(depth, seq * nb, D_PLE)
    pools, convs, ffns = [], [], []
    for layer in range(depth):
        x, npool, nconv, nffn = _prompt_layer(x, p, weights, layer, final=layer == depth - 1, nb=nb)
        pools.append(npool.reshape(POOL_HIST, nb, D_POOL))
        convs.append(nconv.reshape(CONV_HIST, nb, D_CONV))
        ffns.append(nffn.reshape(FFN_HIST, nb, D_FF))
    y_p = jnp.swapaxes(x.reshape(seq, nb, D_MODEL), 0, 1)
    unstack = lambda parts: jnp.swapaxes(jnp.stack(parts), 1, 2)
    return (y_p, y_s, unstack(pools), unstack(convs), unstack(ffns), pool_s, conv_s, ffn_s)
```

```python
import functools
import itertools

import jax
import jax.numpy as jnp
from jax import lax
from jax.experimental import pallas as pl
from jax.experimental.pallas import tpu as pltpu

D_MODEL = 1024
D_POOL = 512
D_CONV = 512
POOL_WINDOWS = (2, 4, 8, 16)
POOL_GROUP = 128
POOL_HIST = 15
CONV_WIDTH = 31
CONV_HIST = 30
D_FF = 2816
FFN_HIST = 2
D_PLE = 256
RMS_EPS = 1e-6
LN_EPS = 1e-5
PAST_LEN = 16384

SUBLANES = 8
VMEM_LIMIT_BYTES = 60 * 1024 * 1024
CONV_PER_FFN = 3

POOL_PAD = 16
CONV_PAD = 32
FFN_CHUNKS = ((0, 512), (512, 512), (1024, 512), (1536, 512), (2048, 512), (2560, 256))
FFN_CHUNK_MAX = 512

BF16 = jnp.bfloat16
F32 = jnp.float32


def _dot(a, b):
    return jnp.dot(a, b, preferred_element_type=F32)


def _rms(x, g):
    return x * lax.rsqrt(jnp.mean(x * x, axis=-1, keepdims=True) + RMS_EPS) * g


def _layernorm(x, g, b):
    mu = jnp.mean(x, axis=-1, keepdims=True)
    xc = x - mu
    var = jnp.mean(xc * xc, axis=-1, keepdims=True)
    return xc * lax.rsqrt(var + LN_EPS) * g + b


def _silu(x):
    return x * jax.nn.sigmoid(x)


def _exact_zero(v):
    bits = pltpu.bitcast(v, jnp.uint32)
    half = jnp.uint32(16)
    zero_bits = lax.shift_right_logical(lax.shift_right_logical(bits, half), half)
    return pltpu.bitcast(zero_bits, F32)


def _mixer_in(x, w):
    n = _rms(x, w["g_mix"][...]).astype(BF16)
    a = _dot(n, w["w_in"][:, 0:D_POOL])
    u1 = _dot(n, w["w_in"][:, D_POOL:D_POOL + D_CONV])
    u2 = _dot(n, w["w_in"][:, D_POOL + D_CONV:D_POOL + 2 * D_CONV])
    return a, u1 * jax.nn.sigmoid(u2)


def _pool_project(zs, w):
    ys = [_dot(z.astype(BF16), w["w_pool"][g]) for g, z in enumerate(zs)]
    return jnp.concatenate(ys, axis=1) * w["pool_scale"][...]


def _conv_post(c, w):
    return _silu(_layernorm(c, w["ln_g"][...], w["ln_b"][...])).astype(BF16)


def _merge(x, ya, cs, w):
    yb = _dot(cs, w["w_pw"][...])
    m = jnp.concatenate([_rms(ya, w["g_out_a"][...]), _rms(yb, w["g_out_b"][...])], axis=1)
    return x + _dot(m.astype(BF16), w["w_out"][...])


def _ffn(x, w, conv_gate, keep_gate):
    n2 = _rms(x, w["g_ffn"][...]).astype(BF16)
    acc = x
    for c0, cf in FFN_CHUNKS:
        gate = _dot(n2, w["w_ffn_in"][:, c0:c0 + cf])
        up = _dot(n2, w["w_ffn_in"][:, D_FF + c0:D_FF + c0 + cf])
        gc = conv_gate(gate, c0, cf)
        keep_gate(gate, c0, cf)
        h = (_silu(gc) * up).astype(BF16)
        acc = acc + _dot(h, w["w_ffn_out"][c0:c0 + cf, :])
    return acc


def _ple(x, p, w):
    e = _rms(_dot(p.astype(BF16), w["w_ple"][...]), w["g_ple"][...])
    return x + jax.nn.sigmoid(_dot(x.astype(BF16), w["w_ple_gate"][...])) * e


WEIGHT_NAMES = ("g_mix", "w_in", "w_pool", "pool_scale", "w_dw", "b_dw", "ln_g", "ln_b", "w_pw",
                "g_out_a", "g_out_b", "w_out", "g_ffn", "w_ffn_in", "w_ffn_dw", "b_ffn_dw",
                "w_ffn_out", "w_ple", "g_ple", "w_ple_gate", "final_norm")
N_W = len(WEIGHT_NAMES)


def _weight_specs(weights, ngrid):
    def spec(arr, pick_layer):
        rest = (0,) * (arr.ndim - 1)
        if pick_layer is None:
            return pl.BlockSpec(arr.shape, lambda *g: (0,) + rest, pipeline_mode=pl.Buffered(1))
        return pl.BlockSpec((None,) + arr.shape[1:], lambda *g: (pick_layer(*g),) + rest,
                            pipeline_mode=pl.Buffered(1))
    return lambda pick_layer: [
        spec(weights[n], None if n == "final_norm" else pick_layer) for n in WEIGHT_NAMES]


def _prompt_layer_kernel(*refs, tt, nb, conv_rows, final):
    x_ref, p_ref = refs[0], refs[1]
    w = dict(zip(WEIGHT_NAMES, refs[2:2 + N_W]))
    y_ref, npool_ref, nconv_ref, nffn_ref = refs[2 + N_W:6 + N_W]
    a_ext, c_ext, cs_ref, g_ext, ffn_hist, xmix, w_rep = refs[6 + N_W:]
    step = pl.program_id(0)
    tm = tt * nb
    pool0 = POOL_PAD * nb
    conv0 = CONV_PAD * nb
    ffn0 = FFN_HIST * nb

    @pl.when(step == 0)
    def _():
        a_ext[0:pool0, :] = jnp.zeros((pool0, D_POOL), F32)
        c_ext[0:conv0, :] = jnp.zeros((conv0, D_CONV), F32)
        ffn_hist[...] = jnp.zeros((ffn0, D_FF), F32)
        xmix[...] = jnp.zeros((tm, D_MODEL), F32)
        for k in range(CONV_WIDTH):
            w_rep[k * SUBLANES:(k + 1) * SUBLANES, :] = jnp.broadcast_to(
                w["w_dw"][k:k + 1, :], (SUBLANES, D_CONV))

    def conv_gate(gate, c0, cf):
        g_ext[0:ffn0, 0:cf] = ffn_hist[:, c0:c0 + cf]
        g_ext[ffn0:ffn0 + tm, 0:cf] = gate
        wd = w["w_ffn_dw"]
        return (wd[0:1, c0:c0 + cf] * g_ext[0:tm, 0:cf]
                + wd[1:2, c0:c0 + cf] * g_ext[nb:nb + tm, 0:cf]
                + wd[2:3, c0:c0 + cf] * gate + w["b_ffn_dw"][:, c0:c0 + cf])

    def keep_gate(gate, c0, cf):
        ffn_hist[:, c0:c0 + cf] = gate[tm - ffn0:tm, :]
        nffn_ref[:, c0:c0 + cf] = gate[tm - ffn0:tm, :]

    def late_stage():
        x1 = xmix[...]
        n2 = _rms(x1, w["g_ffn"][...]).astype(BF16)
        yield
        acc = x1
        for c0, cf in FFN_CHUNKS:
            gate = _dot(n2, w["w_ffn_in"][:, c0:c0 + cf])
            yield
            up = _dot(n2, w["w_ffn_in"][:, D_FF + c0:D_FF + c0 + cf])
            yield
            gc = conv_gate(gate, c0, cf)
            keep_gate(gate, c0, cf)
            h = (_silu(gc) * up).astype(BF16)
            acc = acc + _dot(h, w["w_ffn_out"][c0:c0 + cf, :])
            ffn_done.append(acc[tm - SUBLANES:tm, 0:POOL_GROUP])
            yield
        e = _dot(p_ref[...].astype(BF16), w["w_ple"][...])
        yield
        gate_p = _dot(acc.astype(BF16), w["w_ple_gate"][...])
        yield
        xl = acc + jax.nn.sigmoid(gate_p) * _rms(e, w["g_ple"][...])
        if final:
            xl = _rms(xl, w["final_norm"][...])
        y_ref[...] = xl
        yield

    def early_stage():
        x = x_ref[...]
        a, glu = _mixer_in(x, w)
        a_ext[pool0:pool0 + tm, :] = a
        c_ext[conv0:conv0 + tm, :] = glu
        yield
        pos = step * tt + lax.broadcasted_iota(jnp.int32, (tm, 1), 0) // nb
        zs = []
        for g, win in enumerate(POOL_WINDOWS):
            lo = g * POOL_GROUP
            s = a_ext[pool0 - (win - 1) * nb:pool0 + tm, lo:lo + POOL_GROUP]
            span = 1
            while span < win:
                s = s[span * nb:, :] + s[:-span * nb, :]
                span *= 2
            inv_cnt = 1.0 / jnp.minimum(pos + 1, win).astype(F32)
            zs.append(s * inv_cnt - a[:, lo:lo + POOL_GROUP])
            if g % 2 == 1:
                yield
        ya = _pool_project(zs, w)
        npool_ref[...] = a_ext[tm + pool0 - POOL_HIST * nb:tm + pool0, :]
        a_ext[0:pool0, :] = a_ext[tm:tm + pool0, :]
        base = conv0 - CONV_HIST * nb
        groups = conv_rows // SUBLANES
        prev = None
        for i, r0 in enumerate(range(0, tm, conv_rows)):
            bias = w["b_dw"][...]
            lag = i // CONV_PER_FFN - 1
            if lag >= 0:
                bias = bias + _exact_zero(ffn_done[lag])[0:1, 0:1]
            if prev is not None:
                bias = bias + _exact_zero(prev)[0:1, 0:1]
            acc = jnp.broadcast_to(bias, (groups, SUBLANES, D_CONV))
            for k in range(CONV_WIDTH):
                lo = base + r0 + k * nb
                taps = w_rep[k * SUBLANES:(k + 1) * SUBLANES, :]
                rows = c_ext[lo:lo + conv_rows, :]
                acc = acc + taps[None] * rows.reshape(groups, SUBLANES, D_CONV)
            acc = acc.reshape(conv_rows, D_CONV)
            prev = acc[conv_rows - SUBLANES:conv_rows, 0:POOL_GROUP]
            cs_ref[r0:r0 + conv_rows, :] = _conv_post(acc, w)
            yield
        nconv_ref[...] = c_ext[tm + conv0 - CONV_HIST * nb:tm + conv0, :]
        c_ext[0:conv0, :] = c_ext[tm:tm + conv0, :]
        yb = _dot(cs_ref[...], w["w_pw"][...])
        yield
        m = jnp.concatenate([_rms(ya, w["g_out_a"][...]), _rms(yb, w["g_out_b"][...])], axis=1)
        xmix[...] = x + _dot(m.astype(BF16), w["w_out"][...])
        yield

    ffn_done = []
    late, early = late_stage(), early_stage()
    next(late)
    for _ in itertools.zip_longest(early, late):
        pass


def _prompt_layer(x, p, weights, layer, final, nb, tt=64, conv_rows=32):
    rows = x.shape[0]
    assert nb % SUBLANES == 0 and rows % (tt * nb) == 0
    tm = tt * nb
    last = rows // tm - 1
    early = lambda t: jnp.minimum(t, last)
    late = lambda t: jnp.maximum(t - 1, 0)
    in_specs = [
        pl.BlockSpec((tm, D_MODEL), lambda t: (early(t), 0)),
        pl.BlockSpec((None, tm, D_PLE), lambda t: (layer, late(t), 0)),
    ] + _weight_specs(weights, 1)(lambda t: layer)
    out_shape = (
        jax.ShapeDtypeStruct((rows, D_MODEL), F32),
        jax.ShapeDtypeStruct((POOL_HIST * nb, D_POOL), F32),
        jax.ShapeDtypeStruct((CONV_HIST * nb, D_CONV), F32),
        jax.ShapeDtypeStruct((FFN_HIST * nb, D_FF), F32),
    )
    out_specs = (
        pl.BlockSpec((tm, D_MODEL), lambda t: (late(t), 0)),
        pl.BlockSpec((POOL_HIST * nb, D_POOL), lambda t: (0, 0)),
        pl.BlockSpec((CONV_HIST * nb, D_CONV), lambda t: (0, 0)),
        pl.BlockSpec((FFN_HIST * nb, D_FF), lambda t: (0, 0)),
    )
    scratch = [
        pltpu.VMEM((POOL_PAD * nb + tm, D_POOL), F32),
        pltpu.VMEM((CONV_PAD * nb + tm, D_CONV), F32),
        pltpu.VMEM((tm, D_CONV), BF16),
        pltpu.VMEM((FFN_HIST * nb + tm, FFN_CHUNK_MAX), F32),
        pltpu.VMEM((FFN_HIST * nb, D_FF), F32),
        pltpu.VMEM((tm, D_MODEL), F32),
        pltpu.VMEM((CONV_WIDTH * SUBLANES, D_CONV), F32),
    ]
    return pl.pallas_call(
        functools.partial(_prompt_layer_kernel, tt=tt, nb=nb, conv_rows=conv_rows, final=final),
        grid=(rows // tm + 1,),
        in_specs=in_specs,
        out_specs=out_specs,
        out_shape=out_shape,
        scratch_shapes=scratch,
        compiler_params=pltpu.CompilerParams(
            dimension_semantics=("arbitrary",),
            vmem_limit_bytes=VMEM_LIMIT_BYTES),
        name=f"prompt_layer{layer}",
    )(x, p, *[weights[n] for n in WEIGHT_NAMES])


def _sample_kernel(*refs, sb, steps, depth, conv_rows):
    x_ref, p_ref, sp_ref, sc_ref, sf_ref = refs[0:5]
    w = dict(zip(WEIGHT_NAMES, refs[5:5 + N_W]))
    y_ref, npool_ref, nconv_ref, nffn_ref = refs[5 + N_W:9 + N_W]
    (cs_ref,) = refs[9 + N_W:]
    layer = pl.program_id(0)
    seqs = pl.ds(pl.multiple_of(pl.program_id(1) * sb, sb), sb)
    m = sb * steps

    @pl.when(layer == 0)
    def _():
        y_ref[seqs, :] = x_ref[...]

    x = jnp.concatenate([y_ref[seqs, t * D_MODEL:(t + 1) * D_MODEL] for t in range(steps)], axis=0)
    a, glu = _mixer_in(x, w)

    def pool_ext(i, lo, hi):
        if i < POOL_HIST:
            return sp_ref[:, i * D_POOL + lo:i * D_POOL + hi]
        return a[(i - POOL_HIST) * sb:(i - POOL_HIST + 1) * sb, lo:hi]

    zs = []
    for g, win in enumerate(POOL_WINDOWS):
        lo = g * POOL_GROUP
        rows = []
        for t in range(steps):
            i = POOL_HIST + t
            tok = pool_ext(i, lo, lo + POOL_GROUP)
            s = tok
            for j in range(1, win):
                s = s + pool_ext(i - j, lo, lo + POOL_GROUP)
            cnt = float(min(PAST_LEN + t + 1, win))
            rows.append(s / cnt - tok)
        zs.append(jnp.concatenate(rows, axis=0))
    ya = _pool_project(zs, w)
    for h in range(POOL_HIST):
        npool_ref[:, h * D_POOL:(h + 1) * D_POOL] = pool_ext(h + steps, 0, D_POOL)

    def conv_ext(i, r0):
        if i < CONV_HIST:
            return sc_ref[r0:r0 + conv_rows, i * D_CONV:(i + 1) * D_CONV]
        return glu[(i - CONV_HIST) * sb + r0:(i - CONV_HIST) * sb + r0 + conv_rows, :]

    for t in range(steps):
        for r0 in range(0, sb, conv_rows):
            acc = jnp.broadcast_to(w["b_dw"][...], (conv_rows, D_CONV))
            for k in range(CONV_WIDTH):
                acc = acc + w["w_dw"][k:k + 1, :] * conv_ext(t + k, r0)
            cs_ref[t * sb + r0:t * sb + r0 + conv_rows, :] = _conv_post(acc, w)
    for h in range(CONV_HIST):
        i = h + steps
        if i < CONV_HIST:
            nconv_ref[:, h * D_CONV:(h + 1) * D_CONV] = sc_ref[:, i * D_CONV:(i + 1) * D_CONV]
        else:
            nconv_ref[:, h * D_CONV:(h + 1) * D_CONV] = glu[(i - CONV_HIST) * sb:(i - CONV_HIST + 1) * sb, :]

    x = _merge(x, ya, cs_ref[...], w)

    def conv_gate(gate, c0, cf):
        h0 = sf_ref[:, c0:c0 + cf]
        h1 = sf_ref[:, D_FF + c0:D_FF + c0 + cf]
        prev1 = jnp.concatenate([h1, gate[0:m - sb, :]], axis=0)
        prev2 = jnp.concatenate([h0, h1, gate[0:m - 2 * sb, :]], axis=0)
        wd = w["w_ffn_dw"]
        return (wd[0:1, c0:c0 + cf] * prev2 + wd[1:2, c0:c0 + cf] * prev1
                + wd[2:3, c0:c0 + cf] * gate + w["b_ffn_dw"][:, c0:c0 + cf])

    def keep_gate(gate, c0, cf):
        for h in range(FFN_HIST):
            tt = steps - FFN_HIST + h
            nffn_ref[:, h * D_FF + c0:h * D_FF + c0 + cf] = gate[tt * sb:(tt + 1) * sb, :]

    x = _ffn(x, w, conv_gate, keep_gate)
    p = jnp.concatenate([p_ref[:, t * D_PLE:(t + 1) * D_PLE] for t in range(steps)], axis=0)
    x = _ple(x, p, w)

    @pl.when(layer < depth - 1)
    def _():
        for t in range(steps):
            y_ref[seqs, t * D_MODEL:(t + 1) * D_MODEL] = x[t * sb:(t + 1) * sb, :]

    @pl.when(layer == depth - 1)
    def _():
        y = _rms(x, w["final_norm"][...])
        for t in range(steps):
            y_ref[seqs, t * D_MODEL:(t + 1) * D_MODEL] = y[t * sb:(t + 1) * sb, :]


def _sample_trunk(x, p, st_pool, st_conv, st_ffn, weights, sb=32, conv_rows=32):
    nseq, steps, _ = x.shape
    depth = p.shape[0]
    nblk = nseq // sb
    x2 = x.reshape(nseq, steps * D_MODEL)
    p2 = p.reshape(depth, nseq, steps * D_PLE)
    sp = st_pool.reshape(depth, nseq, POOL_HIST * D_POOL)
    sc = st_conv.reshape(depth, nseq, CONV_HIST * D_CONV)
    sf = st_ffn.reshape(depth, nseq, FFN_HIST * D_FF)

    def state_spec(width):
        return pl.BlockSpec((None, sb, width), lambda l, s: (l, s, 0))

    in_specs = [
        pl.BlockSpec((sb, steps * D_MODEL), lambda l, s: (s, 0)),
        state_spec(steps * D_PLE),
        state_spec(POOL_HIST * D_POOL), state_spec(CONV_HIST * D_CONV), state_spec(FFN_HIST * D_FF),
    ] + _weight_specs(weights, 2)(lambda l, s: l)
    out_shape = (
        jax.ShapeDtypeStruct((nseq, steps * D_MODEL), F32),
        jax.ShapeDtypeStruct(sp.shape, F32),
        jax.ShapeDtypeStruct(sc.shape, F32),
        jax.ShapeDtypeStruct(sf.shape, F32),
    )
    out_specs = (
        pl.BlockSpec((nseq, steps * D_MODEL), lambda l, s: (0, 0)),
        state_spec(POOL_HIST * D_POOL), state_spec(CONV_HIST * D_CONV), state_spec(FFN_HIST * D_FF),
    )
    scratch = [
        pltpu.VMEM((sb * steps, D_CONV), BF16),
    ]
    y, npool, nconv, nffn = pl.pallas_call(
        functools.partial(_sample_kernel, sb=sb, steps=steps, depth=depth, conv_rows=conv_rows),
        grid=(depth, nblk),
        in_specs=in_specs,
        out_specs=out_specs,
        out_shape=out_shape,
        scratch_shapes=scratch,
        compiler_params=pltpu.CompilerParams(
            dimension_semantics=("arbitrary", "arbitrary"),
            vmem_limit_bytes=VMEM_LIMIT_BYTES),
        name="sample_trunk",
    )(x2, p2, sp, sc, sf, *[weights[n] for n in WEIGHT_NAMES])
    return (y.reshape(nseq, steps, D_MODEL),
            npool.reshape(depth, nseq, POOL_HIST, D_POOL),
            nconv.reshape(depth, nseq, CONV_HIST, D_CONV),
            nffn.reshape(depth, nseq, FFN_HIST, D_FF))


def kernel(x_prompt, x_sample, state_pool, state_conv, state_ffn, p_prompt, p_sample, g_mix, w_in, w_pool, pool_scale, w_dw, b_dw, ln_g, ln_b, w_pw, g_out_a, g_out_b, w_out, g_ffn, w_ffn_in, w_ffn_dw, b_ffn_dw, w_ffn_out, w_ple, g_ple, w_ple_gate, final_norm):
    depth = g_mix.shape[0]
    row = lambda v: v.reshape(depth, 1, v.shape[-1])
    weights = {
        "g_mix": row(g_mix), "w_in": w_in.astype(BF16), "w_pool": w_pool.astype(BF16),
        "pool_scale": row(pool_scale), "w_dw": w_dw, "b_dw": row(b_dw), "ln_g": row(ln_g),
        "ln_b": row(ln_b), "w_pw": w_pw.astype(BF16), "g_out_a": row(g_out_a),
        "g_out_b": row(g_out_b), "w_out": w_out.astype(BF16), "g_ffn": row(g_ffn),
        "w_ffn_in": w_ffn_in.astype(BF16), "w_ffn_dw": w_ffn_dw, "b_ffn_dw": row(b_ffn_dw),
        "w_ffn_out": w_ffn_out.astype(BF16), "w_ple": w_ple.astype(BF16), "g_ple": row(g_ple),
        "w_ple_gate": w_ple_gate.astype(BF16), "final_norm": final_norm.reshape(1, D_MODEL),
    }

    y_s, pool_s, conv_s, ffn_s = _sample_trunk(x_sample, p_sample, state_pool, state_conv,
                                               state_ffn, weights)

    nb, seq, _ = x_prompt.shape
    x = jnp.swapaxes(x_prompt, 0, 1).reshape(seq * nb, D_MODEL)
    p = jnp.swapaxes(p_prompt, 1, 2).reshape(depth, seq * nb, D_PLE)
    pools, convs, ffns = [], [], []
    for layer in range(depth):
        x, npool, nconv, nffn = _prompt_layer(x, p, weights, layer, final=layer == depth - 1, nb=nb)
        pools.append(npool.reshape(POOL_HIST, nb, D_POOL))
        convs.append(nconv.reshape(CONV_HIST, nb, D_CONV))
        ffns.append(nffn.reshape(FFN_HIST, nb, D_FF))
    y_p = jnp.swapaxes(x.reshape(seq, nb, D_MODEL), 0, 1)
    unstack = lambda parts: jnp.swapaxes(jnp.stack(parts), 1, 2)
    return (y_p, y_s, unstack(pools), unstack(convs), unstack(ffns), pool_s, conv_s, ffn_s)
```

```python
import functools
import itertools

import jax
import jax.numpy as jnp
from jax import lax
from jax.experimental import pallas as pl
from jax.experimental.pallas import tpu as pltpu

D_MODEL = 1024
D_POOL = 512
D_CONV = 512
POOL_WINDOWS = (2, 4, 8, 16)
POOL_GROUP = 128
POOL_HIST = 15
CONV_WIDTH = 31
CONV_HIST = 30
D_FF = 2816
FFN_HIST = 2
D_PLE = 256
RMS_EPS = 1e-6
LN_EPS = 1e-5
PAST_LEN = 16384

SUBLANES = 8
VMEM_LIMIT_BYTES = 60 * 1024 * 1024
CONV_PER_FFN = 3

POOL_PAD = 16
CONV_PAD = 32
FFN_CHUNKS = ((0, 512), (512, 512), (1024, 512), (1536, 512), (2048, 512), (2560, 256))
FFN_CHUNK_MAX = 512

BF16 = jnp.bfloat16
F32 = jnp.float32


def _dot(a, b):
    return jnp.dot(a, b, preferred_element_type=F32)


def _rms(x, g):
    return x * lax.rsqrt(jnp.mean(x * x, axis=-1, keepdims=True) + RMS_EPS) * g


def _layernorm(x, g, b):
    mu = jnp.mean(x, axis=-1, keepdims=True)
    xc = x - mu
    var = jnp.mean(xc * xc, axis=-1, keepdims=True)
    return xc * lax.rsqrt(var + LN_EPS) * g + b


def _silu(x):
    return x * jax.nn.sigmoid(x)


def _exact_zero(v):
    bits = pltpu.bitcast(v, jnp.uint32)
    half = jnp.uint32(16)
    zero_bits = lax.shift_right_logical(lax.shift_right_logical(bits, half), half)
    return pltpu.bitcast(zero_bits, F32)


def _mixer_in(x, w):
    n = _rms(x, w["g_mix"][...]).astype(BF16)
    a = _dot(n, w["w_in"][:, 0:D_POOL])
    u1 = _dot(n, w["w_in"][:, D_POOL:D_POOL + D_CONV])
    u2 = _dot(n, w["w_in"][:, D_POOL + D_CONV:D_POOL + 2 * D_CONV])
    return a, u1 * jax.nn.sigmoid(u2)


def _pool_project(zs, w):
    ys = [_dot(z.astype(BF16), w["w_pool"][g]) for g, z in enumerate(zs)]
    return jnp.concatenate(ys, axis=1) * w["pool_scale"][...]


def _conv_post(c, w):
    return _silu(_layernorm(c, w["ln_g"][...], w["ln_b"][...])).astype(BF16)


def _merge(x, ya, cs, w):
    yb = _dot(cs, w["w_pw"][...])
    m = jnp.concatenate([_rms(ya, w["g_out_a"][...]), _rms(yb, w["g_out_b"][...])], axis=1)
    return x + _dot(m.astype(BF16), w["w_out"][...])


def _ffn(x, w, conv_gate, keep_gate):
    n2 = _rms(x, w["g_ffn"][...]).astype(BF16)
    acc = x
    for c0, cf in FFN_CHUNKS:
        gate = _dot(n2, w["w_ffn_in"][:, c0:c0 + cf])
        up = _dot(n2, w["w_ffn_in"][:, D_FF + c0:D_FF + c0 + cf])
        gc = conv_gate(gate, c0, cf)
        keep_gate(gate, c0, cf)
        h = (_silu(gc) * up).astype(BF16)
        acc = acc + _dot(h, w["w_ffn_out"][c0:c0 + cf, :])
    return acc


def _ple(x, p, w):
    e = _rms(_dot(p.astype(BF16), w["w_ple"][...]), w["g_ple"][...])
    return x + jax.nn.sigmoid(_dot(x.astype(BF16), w["w_ple_gate"][...])) * e


WEIGHT_NAMES = ("g_mix", "w_in", "w_pool", "pool_scale", "w_dw", "b_dw", "ln_g", "ln_b", "w_pw",
                "g_out_a", "g_out_b", "w_out", "g_ffn", "w_ffn_in", "w_ffn_dw", "b_ffn_dw",
                "w_ffn_out", "w_ple", "g_ple", "w_ple_gate", "final_norm")
N_W = len(WEIGHT_NAMES)


def _weight_specs(weights, ngrid):
    def spec(arr, pick_layer):
        rest = (0,) * (arr.ndim - 1)
        if pick_layer is None:
            return pl.BlockSpec(arr.shape, lambda *g: (0,) + rest, pipeline_mode=pl.Buffered(1))
        return pl.BlockSpec((None,) + arr.shape[1:], lambda *g: (pick_layer(*g),) + rest,
                            pipeline_mode=pl.Buffered(1))
    return lambda pick_layer: [
        spec(weights[n], None if n == "final_norm" else pick_layer) for n in WEIGHT_NAMES]


def _prompt_layer_kernel(*refs, tt, nb, conv_rows, final):
    x_ref, p_ref = refs[0], refs[1]
    w = dict(zip(WEIGHT_NAMES, refs[2:2 + N_W]))
    y_ref, npool_ref, nconv_ref, nffn_ref = refs[2 + N_W:6 + N_W]
    a_ext, c_ext, cs_ref, g_ext, ffn_hist, xmix, w_rep = refs[6 + N_W:]
    step = pl.program_id(0)
    tm = tt * nb
    pool0 = POOL_PAD * nb
    conv0 = CONV_PAD * nb
    ffn0 = FFN_HIST * nb

    @pl.when(step == 0)
    def _():
        a_ext[0:pool0, :] = jnp.zeros((pool0, D_POOL), F32)
        c_ext[0:conv0, :] = jnp.zeros((conv0, D_CONV), F32)
        ffn_hist[...] = jnp.zeros((ffn0, D_FF), F32)
        xmix[...] = jnp.zeros((tm, D_MODEL), F32)
        for k in range(CONV_WIDTH):
            w_rep[k * SUBLANES:(k + 1) * SUBLANES, :] = jnp.broadcast_to(
                w["w_dw"][k:k + 1, :], (SUBLANES, D_CONV))

    def conv_gate(gate, c0, cf):
        g_ext[0:ffn0, 0:cf] = ffn_hist[:, c0:c0 + cf]
        g_ext[ffn0:ffn0 + tm, 0:cf] = gate
        wd = w["w_ffn_dw"]
        return (wd[0:1, c0:c0 + cf] * g_ext[0:tm, 0:cf]
                + wd[1:2, c0:c0 + cf] * g_ext[nb:nb + tm, 0:cf]
                + wd[2:3, c0:c0 + cf] * gate + w["b_ffn_dw"][:, c0:c0 + cf])

    def keep_gate(gate, c0, cf):
        ffn_hist[:, c0:c0 + cf] = gate[tm - ffn0:tm, :]
        nffn_ref[:, c0:c0 + cf] = gate[tm - ffn0:tm, :]

    def late_stage():
        x1 = xmix[...]
        n2 = _rms(x1, w["g_ffn"][...]).astype(BF16)
        yield
        acc = x1
        for c0, cf in FFN_CHUNKS:
            gate = _dot(n2, w["w_ffn_in"][:, c0:c0 + cf])
            yield
            up = _dot(n2, w["w_ffn_in"][:, D_FF + c0:D_FF + c0 + cf])
            yield
            gc = conv_gate(gate, c0, cf)
            keep_gate(gate, c0, cf)
            h = (_silu(gc) * up).astype(BF16)
            acc = acc + _dot(h, w["w_ffn_out"][c0:c0 + cf, :])
            ffn_done.append(acc[tm - SUBLANES:tm, 0:POOL_GROUP])
            yield
        e = _dot(p_ref[...].astype(BF16), w["w_ple"][...])
        yield
        gate_p = _dot(acc.astype(BF16), w["w_ple_gate"][...])
        yield
        xl = acc + jax.nn.sigmoid(gate_p) * _rms(e, w["g_ple"][...])
        if final:
            xl = _rms(xl, w["final_norm"][...])
        y_ref[...] = xl
        yield

    def early_stage():
        x = x_ref[...]
        a, glu = _mixer_in(x, w)
        a_ext[pool0:pool0 + tm, :] = a
        c_ext[conv0:conv0 + tm, :] = glu
        yield
        pos = step * tt + lax.broadcasted_iota(jnp.int32, (tm, 1), 0) // nb
        zs = []
        for g, win in enumerate(POOL_WINDOWS):
            lo = g * POOL_GROUP
            s = a_ext[pool0 - (win - 1) * nb:pool0 + tm, lo:lo + POOL_GROUP]
            span = 1
            while span < win:
                s = s[span * nb:, :] + s[:-span * nb, :]
                span *= 2
            inv_cnt = 1.0 / jnp.minimum(pos + 1, win).astype(F32)
            zs.append(s * inv_cnt - a[:, lo:lo + POOL_GROUP])
            if g % 2 == 1:
                yield
        ya = _pool_project(zs, w)
        npool_ref[...] = a_ext[tm + pool0 - POOL_HIST * nb:tm + pool0, :]
        a_ext[0:pool0, :] = a_ext[tm:tm + pool0, :]
        base = conv0 - CONV_HIST * nb
        groups = conv_rows // SUBLANES
        prev = None
        for i, r0 in enumerate(range(0, tm, conv_rows)):
            bias = w["b_dw"][...]
            lag = i // CONV_PER_FFN - 1
            if lag >= 0:
                bias = bias + _exact_zero(ffn_done[lag])[0:1, 0:1]
            if prev is not None:
                bias = bias + _exact_zero(prev)[0:1, 0:1]
            acc = jnp.broadcast_to(bias, (groups, SUBLANES, D_CONV))
            for k in range(CONV_WIDTH):
                lo = base + r0 + k * nb
                taps = w_rep[k * SUBLANES:(k + 1) * SUBLANES, :]
                rows = c_ext[lo:lo + conv_rows, :]
                acc = acc + taps[None] * rows.reshape(groups, SUBLANES, D_CONV)
            acc = acc.reshape(conv_rows, D_CONV)
            prev = acc[conv_rows - SUBLANES:conv_rows, 0:POOL_GROUP]
            cs_ref[r0:r0 + conv_rows, :] = _conv_post(acc, w)
            yield
        nconv_ref[...] = c_ext[tm + conv0 - CONV_HIST * nb:tm + conv0, :]
        c_ext[0:conv0, :] = c_ext[tm:tm + conv0, :]
        yb = _dot(cs_ref[...], w["w_pw"][...])
        yield
        m = jnp.concatenate([_rms(ya, w["g_out_a"][...]), _rms(yb, w["g_out_b"][...])], axis=1)
        xmix[...] = x + _dot(m.astype(BF16), w["w_out"][...])
        yield

    ffn_done = []
    late, early = late_stage(), early_stage()
    next(late)
    for _ in itertools.zip_longest(early, late):
        pass


def _prompt_layer(x, p, weights, layer, final, nb, tt=64, conv_rows=32):
    rows = x.shape[0]
    assert nb % SUBLANES == 0 and rows % (tt * nb) == 0
    tm = tt * nb
    last = rows // tm - 1
    early = lambda t: jnp.minimum(t, last)
    late = lambda t: jnp.maximum(t - 1, 0)
    in_specs = [
        pl.BlockSpec((tm, D_MODEL), lambda t: (early(t), 0)),
        pl.BlockSpec((None, tm, D_PLE), lambda t: (layer, late(t), 0)),
    ] + _weight_specs(weights, 1)(lambda t: layer)
    out_shape = (
        jax.ShapeDtypeStruct((rows, D_MODEL), F32),
        jax.ShapeDtypeStruct((POOL_HIST * nb, D_POOL), F32),
        jax.ShapeDtypeStruct((CONV_HIST * nb, D_CONV), F32),
        jax.ShapeDtypeStruct((FFN_HIST * nb, D_FF), F32),
    )
    out_specs = (
        pl.BlockSpec((tm, D_MODEL), lambda t: (late(t), 0)),
        pl.BlockSpec((POOL_HIST * nb, D_POOL), lambda t: (0, 0)),
        pl.BlockSpec((CONV_HIST * nb, D_CONV), lambda t: (0, 0)),
        pl.BlockSpec((FFN_HIST * nb, D_FF), lambda t: (0, 0)),
    )
    scratch = [
        pltpu.VMEM((POOL_PAD * nb + tm, D_POOL), F32),
        pltpu.VMEM((CONV_PAD * nb + tm, D_CONV), F32),
        pltpu.VMEM((tm, D_CONV), BF16),
        pltpu.VMEM((FFN_HIST * nb + tm, FFN_CHUNK_MAX), F32),
        pltpu.VMEM((FFN_HIST * nb, D_FF), F32),
        pltpu.VMEM((tm, D_MODEL), F32),
        pltpu.VMEM((CONV_WIDTH * SUBLANES, D_CONV), F32),
    ]
    return pl.pallas_call(
        functools.partial(_prompt_layer_kernel, tt=tt, nb=nb, conv_rows=conv_rows, final=final),
        grid=(rows // tm + 1,),
        in_specs=in_specs,
        out_specs=out_specs,
        out_shape=out_shape,
        scratch_shapes=scratch,
        compiler_params=pltpu.CompilerParams(
            dimension_semantics=("arbitrary",),
            vmem_limit_bytes=VMEM_LIMIT_BYTES),
        name=f"prompt_layer{layer}",
    )(x, p, *[weights[n] for n in WEIGHT_NAMES])


def _sample_kernel(*refs, sb, steps, depth, conv_rows):
    x_ref, p_ref, sp_ref, sc_ref, sf_ref = refs[0:5]
    w = dict(zip(WEIGHT_NAMES, refs[5:5 + N_W]))
    y_ref, npool_ref, nconv_ref, nffn_ref = refs[5 + N_W:9 + N_W]
    (cs_ref,) = refs[9 + N_W:]
    layer = pl.program_id(0)
    seqs = pl.ds(pl.multiple_of(pl.program_id(1) * sb, sb), sb)
    m = sb * steps

    @pl.when(layer == 0)
    def _():
        y_ref[seqs, :, :] = x_ref[...]

    x = jnp.concatenate([y_ref[seqs, t, :] for t in range(steps)], axis=0)
    a, glu = _mixer_in(x, w)

    def pool_ext(i, lo, hi):
        if i < POOL_HIST:
            return sp_ref[:, i, lo:hi]
        return a[(i - POOL_HIST) * sb:(i - POOL_HIST + 1) * sb, lo:hi]

    zs = []
    for g, win in enumerate(POOL_WINDOWS):
        lo = g * POOL_GROUP
        rows = []
        for t in range(steps):
            i = POOL_HIST + t
            tok = pool_ext(i, lo, lo + POOL_GROUP)
            s = tok
            for j in range(1, win):
                s = s + pool_ext(i - j, lo, lo + POOL_GROUP)
            cnt = float(min(PAST_LEN + t + 1, win))
            rows.append(s / cnt - tok)
        zs.append(jnp.concatenate(rows, axis=0))
    ya = _pool_project(zs, w)
    for h in range(POOL_HIST):
        npool_ref[:, h, :] = pool_ext(h + steps, 0, D_POOL)

    def conv_ext(i, r0):
        if i < CONV_HIST:
            return sc_ref[r0:r0 + conv_rows, i, :]
        return glu[(i - CONV_HIST) * sb + r0:(i - CONV_HIST) * sb + r0 + conv_rows, :]

    for t in range(steps):
        for r0 in range(0, sb, conv_rows):
            acc = jnp.broadcast_to(w["b_dw"][...], (conv_rows, D_CONV))
            for k in range(CONV_WIDTH):
                acc = acc + w["w_dw"][k:k + 1, :] * conv_ext(t + k, r0)
            cs_ref[t * sb + r0:t * sb + r0 + conv_rows, :] = _conv_post(acc, w)
    for h in range(CONV_HIST):
        i = h + steps
        if i < CONV_HIST:
            nconv_ref[:, h, :] = sc_ref[:, i, :]
        else:
            nconv_ref[:, h, :] = glu[(i - CONV_HIST) * sb:(i - CONV_HIST + 1) * sb, :]

    x = _merge(x, ya, cs_ref[...], w)

    def conv_gate(gate, c0, cf):
        h0 = sf_ref[:, c0:c0 + cf]
        h1 = sf_ref[:, D_FF + c0:D_FF + c0 + cf]
        prev1 = jnp.concatenate([h1, gate[0:m - sb, :]], axis=0)
        prev2 = jnp.concatenate([h0, h1, gate[0:m - 2 * sb, :]], axis=0)
        wd = w["w_ffn_dw"]
        return (wd[0:1, c0:c0 + cf] * prev2 + wd[1:2, c0:c0 + cf] * prev1
                + wd[2:3, c0:c0 + cf] * gate + w["b_ffn_dw"][:, c0:c0 + cf])

    def keep_gate(gate, c0, cf):
        for h in range(FFN_HIST):
            tt = steps - FFN_HIST + h
            nffn_ref[:, h * D_FF + c0:h * D_FF + c0 + cf] = gate[tt * sb:(tt + 1) * sb, :]

    x = _ffn(x, w, conv_gate, keep_gate)
    p = jnp.concatenate([p_ref[:, t, :] for t in range(steps)], axis=0)
    x = _ple(x, p, w)

    @pl.when(layer < depth - 1)
    def _():
        for t in range(steps):
            y_ref[seqs, t, :] = x[t * sb:(t + 1) * sb, :]

    @pl.when(layer == depth - 1)
    def _():
        y = _rms(x, w["final_norm"][...])
        for t in range(steps):
            y_ref[seqs, t, :] = y[t * sb:(t + 1) * sb, :]


def _sample_trunk(x, p, st_pool, st_conv, st_ffn, weights, sb=32, conv_rows=32):
    nseq, steps, _ = x.shape
    depth = p.shape[0]
    nblk = nseq // sb
    sf = st_ffn.reshape(depth, nseq, FFN_HIST * D_FF)

    def state_spec(hist, width):
        return pl.BlockSpec((None, sb, hist, width), lambda l, s: (l, s, 0, 0))

    ffn_spec = pl.BlockSpec((None, sb, FFN_HIST * D_FF), lambda l, s: (l, s, 0))
    in_specs = [
        pl.BlockSpec((sb, steps, D_MODEL), lambda l, s: (s, 0, 0)),
        state_spec(steps, D_PLE),
        state_spec(POOL_HIST, D_POOL), state_spec(CONV_HIST, D_CONV), ffn_spec,
    ] + _weight_specs(weights, 2)(lambda l, s: l)
    out_shape = (
        jax.ShapeDtypeStruct(x.shape, F32),
        jax.ShapeDtypeStruct(st_pool.shape, F32),
        jax.ShapeDtypeStruct(st_conv.shape, F32),
        jax.ShapeDtypeStruct(sf.shape, F32),
    )
    out_specs = (
        pl.BlockSpec((nseq, steps, D_MODEL), lambda l, s: (0, 0, 0)),
        state_spec(POOL_HIST, D_POOL), state_spec(CONV_HIST, D_CONV), ffn_spec,
    )
    scratch = [
        pltpu.VMEM((sb * steps, D_CONV), BF16),
    ]
    y, npool, nconv, nffn = pl.pallas_call(
        functools.partial(_sample_kernel, sb=sb, steps=steps, depth=depth, conv_rows=conv_rows),
        grid=(depth, nblk),
        in_specs=in_specs,
        out_specs=out_specs,
        out_shape=out_shape,
        scratch_shapes=scratch,
        compiler_params=pltpu.CompilerParams(
            dimension_semantics=("arbitrary", "arbitrary"),
            vmem_limit_bytes=VMEM_LIMIT_BYTES),
        name="sample_trunk",
    )(x, p, st_pool, st_conv, sf, *[weights[n] for n in WEIGHT_NAMES])
    return y, npool, nconv, nffn.reshape(depth, nseq, FFN_HIST, D_FF)


def kernel(x_prompt, x_sample, state_pool, state_conv, state_ffn, p_prompt, p_sample, g_mix, w_in, w_pool, pool_scale, w_dw, b_dw, ln_g, ln_b, w_pw, g_out_a, g_out_b, w_out, g_ffn, w_ffn_in, w_ffn_dw, b_ffn_dw, w_ffn_out, w_ple, g_ple, w_ple_gate, final_norm):
    depth = g_mix.shape[0]
    row = lambda v: v.reshape(depth, 1, v.shape[-1])
    weights = {
        "g_mix": row(g_mix), "w_in": w_in.astype(BF16), "w_pool": w_pool.astype(BF16),
        "pool_scale": row(pool_scale), "w_dw": w_dw, "b_dw": row(b_dw), "ln_g": row(ln_g),
        "ln_b": row(ln_b), "w_pw": w_pw.astype(BF16), "g_out_a": row(g_out_a),
        "g_out_b": row(g_out_b), "w_out": w_out.astype(BF16), "g_ffn": row(g_ffn),
        "w_ffn_in": w_ffn_in.astype(BF16), "w_ffn_dw": w_ffn_dw, "b_ffn_dw": row(b_ffn_dw),
        "w_ffn_out": w_ffn_out.astype(BF16), "w_ple": w_ple.astype(BF16), "g_ple": row(g_ple),
        "w_ple_gate": w_ple_gate.astype(BF16), "final_norm": final_norm.reshape(1, D_MODEL),
    }

    y_s, pool_s, conv_s, ffn_s = _sample_trunk(x_sample, p_sample, state_pool, state_conv,
                                               state_ffn, weights)

    nb, seq, _ = x_prompt.shape
    x = jnp.swapaxes(x_prompt, 0, 1).reshape(seq * nb, D_MODEL)
    p = jnp.swapaxes(p_prompt, 1, 2).reshape(depth, seq * nb, D_PLE)
    pools, convs, ffns = [], [], []
    for layer in range(depth):
        x, npool, nconv, nffn = _prompt_layer(x, p, weights, layer, final=layer == depth - 1, nb=nb)
        pools.append(npool.reshape(POOL_HIST, nb, D_POOL))
        convs.append(nconv.reshape(CONV_HIST, nb, D_CONV))
        ffns.append(nffn.reshape(FFN_HIST, nb, D_FF))
    y_p = jnp.swapaxes(x.reshape(seq, nb, D_MODEL), 0, 1)
    unstack = lambda parts: jnp.swapaxes(jnp.stack(parts), 1, 2)
    return (y_p, y_s, unstack(pools), unstack(convs), unstack(ffns), pool_s, conv_s, ffn_s)
```

```python
import functools
import itertools

import jax
import jax.numpy as jnp
from jax import lax
from jax.experimental import pallas as pl
from jax.experimental.pallas import tpu as pltpu

D_MODEL = 1024
D_POOL = 512
D_CONV = 512
POOL_WINDOWS = (2, 4, 8, 16)
POOL_GROUP = 128
POOL_HIST = 15
CONV_WIDTH = 31
CONV_HIST = 30
D_FF = 2816
FFN_HIST = 2
D_PLE = 256
RMS_EPS = 1e-6
LN_EPS = 1e-5
PAST_LEN = 16384

SUBLANES = 8
VMEM_LIMIT_BYTES = 60 * 1024 * 1024
CONV_PER_FFN = 3

POOL_PAD = 16
CONV_PAD = 32
FFN_CHUNKS = ((0, 512), (512, 512), (1024, 512), (1536, 512), (2048, 512), (2560, 256))
FFN_CHUNK_MAX = 512

BF16 = jnp.bfloat16
F32 = jnp.float32


def _dot(a, b):
    return jnp.dot(a, b, preferred_element_type=F32)


def _rms(x, g):
    return x * lax.rsqrt(jnp.mean(x * x, axis=-1, keepdims=True) + RMS_EPS) * g


def _layernorm(x, g, b):
    mu = jnp.mean(x, axis=-1, keepdims=True)
    xc = x - mu
    var = jnp.mean(xc * xc, axis=-1, keepdims=True)
    return xc * lax.rsqrt(var + LN_EPS) * g + b


def _silu(x):
    return x * jax.nn.sigmoid(x)


def _exact_zero(v):
    bits = pltpu.bitcast(v, jnp.uint32)
    half = jnp.uint32(16)
    zero_bits = lax.shift_right_logical(lax.shift_right_logical(bits, half), half)
    return pltpu.bitcast(zero_bits, F32)


def _mixer_in(x, w):
    n = _rms(x, w["g_mix"][...]).astype(BF16)
    a = _dot(n, w["w_in"][:, 0:D_POOL])
    u1 = _dot(n, w["w_in"][:, D_POOL:D_POOL + D_CONV])
    u2 = _dot(n, w["w_in"][:, D_POOL + D_CONV:D_POOL + 2 * D_CONV])
    return a, u1 * jax.nn.sigmoid(u2)


def _pool_project(zs, w):
    ys = [_dot(z.astype(BF16), w["w_pool"][g]) for g, z in enumerate(zs)]
    return jnp.concatenate(ys, axis=1) * w["pool_scale"][...]


def _conv_post(c, w):
    return _silu(_layernorm(c, w["ln_g"][...], w["ln_b"][...])).astype(BF16)


def _merge(x, ya, cs, w):
    yb = _dot(cs, w["w_pw"][...])
    m = jnp.concatenate([_rms(ya, w["g_out_a"][...]), _rms(yb, w["g_out_b"][...])], axis=1)
    return x + _dot(m.astype(BF16), w["w_out"][...])


def _ffn(x, w, conv_gate, keep_gate):
    n2 = _rms(x, w["g_ffn"][...]).astype(BF16)
    acc = x
    for c0, cf in FFN_CHUNKS:
        gate = _dot(n2, w["w_ffn_in"][:, c0:c0 + cf])
        up = _dot(n2, w["w_ffn_in"][:, D_FF + c0:D_FF + c0 + cf])
        gc = conv_gate(gate, c0, cf)
        keep_gate(gate, c0, cf)
        h = (_silu(gc) * up).astype(BF16)
        acc = acc + _dot(h, w["w_ffn_out"][c0:c0 + cf, :])
    return acc


def _ple(x, p, w):
    e = _rms(_dot(p.astype(BF16), w["w_ple"][...]), w["g_ple"][...])
    return x + jax.nn.sigmoid(_dot(x.astype(BF16), w["w_ple_gate"][...])) * e


WEIGHT_NAMES = ("g_mix", "w_in", "w_pool", "pool_scale", "w_dw", "b_dw", "ln_g", "ln_b", "w_pw",
                "g_out_a", "g_out_b", "w_out", "g_ffn", "w_ffn_in", "w_ffn_dw", "b_ffn_dw",
                "w_ffn_out", "w_ple", "g_ple", "w_ple_gate", "final_norm")
N_W = len(WEIGHT_NAMES)


def _weight_specs(weights, ngrid):
    def spec(arr, pick_layer):
        rest = (0,) * (arr.ndim - 1)
        if pick_layer is None:
            return pl.BlockSpec(arr.shape, lambda *g: (0,) + rest, pipeline_mode=pl.Buffered(1))
        return pl.BlockSpec((None,) + arr.shape[1:], lambda *g: (pick_layer(*g),) + rest,
                            pipeline_mode=pl.Buffered(1))
    return lambda pick_layer: [
        spec(weights[n], None if n == "final_norm" else pick_layer) for n in WEIGHT_NAMES]


def _prompt_layer_kernel(*refs, tt, nb, conv_rows, final):
    x_ref, p_ref = refs[0], refs[1]
    w = dict(zip(WEIGHT_NAMES, refs[2:2 + N_W]))
    y_ref, npool_ref, nconv_ref, nffn_ref = refs[2 + N_W:6 + N_W]
    a_ext, c_ext, cs_ref, g_ext, ffn_hist, xmix, w_rep = refs[6 + N_W:]
    step = pl.program_id(0)
    tm = tt * nb
    pool0 = POOL_PAD * nb
    conv0 = CONV_PAD * nb
    ffn0 = FFN_HIST * nb

    @pl.when(step == 0)
    def _():
        a_ext[0:pool0, :] = jnp.zeros((pool0, D_POOL), F32)
        c_ext[0:conv0, :] = jnp.zeros((conv0, D_CONV), F32)
        ffn_hist[...] = jnp.zeros((ffn0, D_FF), F32)
        xmix[...] = jnp.zeros((tm, D_MODEL), F32)
        for k in range(CONV_WIDTH):
            w_rep[k * SUBLANES:(k + 1) * SUBLANES, :] = jnp.broadcast_to(
                w["w_dw"][k:k + 1, :], (SUBLANES, D_CONV))

    def conv_gate(gate, c0, cf):
        g_ext[0:ffn0, 0:cf] = ffn_hist[:, c0:c0 + cf]
        g_ext[ffn0:ffn0 + tm, 0:cf] = gate
        wd = w["w_ffn_dw"]
        return (wd[0:1, c0:c0 + cf] * g_ext[0:tm, 0:cf]
                + wd[1:2, c0:c0 + cf] * g_ext[nb:nb + tm, 0:cf]
                + wd[2:3, c0:c0 + cf] * gate + w["b_ffn_dw"][:, c0:c0 + cf])

    def keep_gate(gate, c0, cf):
        ffn_hist[:, c0:c0 + cf] = gate[tm - ffn0:tm, :]
        nffn_ref[:, c0:c0 + cf] = gate[tm - ffn0:tm, :]

    def late_stage():
        x1 = xmix[...]
        n2 = _rms(x1, w["g_ffn"][...]).astype(BF16)
        yield
        acc = x1
        for c0, cf in FFN_CHUNKS:
            gate = _dot(n2, w["w_ffn_in"][:, c0:c0 + cf])
            yield
            up = _dot(n2, w["w_ffn_in"][:, D_FF + c0:D_FF + c0 + cf])
            yield
            gc = conv_gate(gate, c0, cf)
            keep_gate(gate, c0, cf)
            h = (_silu(gc) * up).astype(BF16)
            acc = acc + _dot(h, w["w_ffn_out"][c0:c0 + cf, :])
            ffn_done.append(acc[tm - SUBLANES:tm, 0:POOL_GROUP])
            yield
        e = _dot(p_ref[...].astype(BF16), w["w_ple"][...])
        yield
        gate_p = _dot(acc.astype(BF16), w["w_ple_gate"][...])
        yield
        xl = acc + jax.nn.sigmoid(gate_p) * _rms(e, w["g_ple"][...])
        if final:
            xl = _rms(xl, w["final_norm"][...])
        y_ref[...] = xl
        yield

    def early_stage():
        x = x_ref[...]
        a, glu = _mixer_in(x, w)
        a_ext[pool0:pool0 + tm, :] = a
        c_ext[conv0:conv0 + tm, :] = glu
        yield
        pos = step * tt + lax.broadcasted_iota(jnp.int32, (tm, 1), 0) // nb
        zs = []
        for g, win in enumerate(POOL_WINDOWS):
            lo = g * POOL_GROUP
            s = a_ext[pool0 - (win - 1) * nb:pool0 + tm, lo:lo + POOL_GROUP]
            span = 1
            while span < win:
                s = s[span * nb:, :] + s[:-span * nb, :]
                span *= 2
            inv_cnt = 1.0 / jnp.minimum(pos + 1, win).astype(F32)
            zs.append(s * inv_cnt - a[:, lo:lo + POOL_GROUP])
            if g % 2 == 1:
                yield
        ya = _pool_project(zs, w)
        npool_ref[...] = a_ext[tm + pool0 - POOL_HIST * nb:tm + pool0, :]
        a_ext[0:pool0, :] = a_ext[tm:tm + pool0, :]
        base = conv0 - CONV_HIST * nb
        groups = conv_rows // SUBLANES
        prev = None
        for i, r0 in enumerate(range(0, tm, conv_rows)):
            bias = w["b_dw"][...]
            lag = i // CONV_PER_FFN - 1
            if lag >= 0:
                bias = bias + _exact_zero(ffn_done[lag])[0:1, 0:1]
            if prev is not None:
                bias = bias + _exact_zero(prev)[0:1, 0:1]
            acc = jnp.broadcast_to(bias, (groups, SUBLANES, D_CONV))
            for k in range(CONV_WIDTH):
                lo = base + r0 + k * nb
                taps = w_rep[k * SUBLANES:(k + 1) * SUBLANES, :]
                rows = c_ext[lo:lo + conv_rows, :]
                acc = acc + taps[None] * rows.reshape(groups, SUBLANES, D_CONV)
            acc = acc.reshape(conv_rows, D_CONV)
            prev = acc[conv_rows - SUBLANES:conv_rows, 0:POOL_GROUP]
            cs_ref[r0:r0 + conv_rows, :] = _conv_post(acc, w)
            yield
        nconv_ref[...] = c_ext[tm + conv0 - CONV_HIST * nb:tm + conv0, :]
        c_ext[0:conv0, :] = c_ext[tm:tm + conv0, :]
        yb = _dot(cs_ref[...], w["w_pw"][...])
        yield
        m = jnp.concatenate([_rms(ya, w["g_out_a"][...]), _rms(yb, w["g_out_b"][...])], axis=1)
        xmix[...] = x + _dot(m.astype(BF16), w["w_out"][...])
        yield

    ffn_done = []
    late, early = late_stage(), early_stage()
    next(late)
    for _ in itertools.zip_longest(early, late):
        pass


def _prompt_layer(x, p, weights, layer, final, nb, tt=64, conv_rows=32):
    rows = x.shape[0]
    assert nb % SUBLANES == 0 and rows % (tt * nb) == 0
    tm = tt * nb
    last = rows // tm - 1
    early = lambda t: jnp.minimum(t, last)
    late = lambda t: jnp.maximum(t - 1, 0)
    in_specs = [
        pl.BlockSpec((tm, D_MODEL), lambda t: (early(t), 0)),
        pl.BlockSpec((None, tm, D_PLE), lambda t: (layer, late(t), 0)),
    ] + _weight_specs(weights, 1)(lambda t: layer)
    out_shape = (
        jax.ShapeDtypeStruct((rows, D_MODEL), F32),
        jax.ShapeDtypeStruct((POOL_HIST * nb, D_POOL), F32),
        jax.ShapeDtypeStruct((CONV_HIST * nb, D_CONV), F32),
        jax.ShapeDtypeStruct((FFN_HIST * nb, D_FF), F32),
    )
    out_specs = (
        pl.BlockSpec((tm, D_MODEL), lambda t: (late(t), 0)),
        pl.BlockSpec((POOL_HIST * nb, D_POOL), lambda t: (0, 0)),
        pl.BlockSpec((CONV_HIST * nb, D_CONV), lambda t: (0, 0)),
        pl.BlockSpec((FFN_HIST * nb, D_FF), lambda t: (0, 0)),
    )
    scratch = [
        pltpu.VMEM((POOL_PAD * nb + tm, D_POOL), F32),
        pltpu.VMEM((CONV_PAD * nb + tm, D_CONV), F32),
        pltpu.VMEM((tm, D_CONV), BF16),
        pltpu.VMEM((FFN_HIST * nb + tm, FFN_CHUNK_MAX), F32),
        pltpu.VMEM((FFN_HIST * nb, D_FF), F32),
        pltpu.VMEM((tm, D_MODEL), F32),
        pltpu.VMEM((CONV_WIDTH * SUBLANES, D_CONV), F32),
    ]
    return pl.pallas_call(
        functools.partial(_prompt_layer_kernel, tt=tt, nb=nb, conv_rows=conv_rows, final=final),
        grid=(rows // tm + 1,),
        in_specs=in_specs,
        out_specs=out_specs,
        out_shape=out_shape,
        scratch_shapes=scratch,
        compiler_params=pltpu.CompilerParams(
            dimension_semantics=("arbitrary",),
            vmem_limit_bytes=VMEM_LIMIT_BYTES),
        name=f"prompt_layer{layer}",
    )(x, p, *[weights[n] for n in WEIGHT_NAMES])


def _sample_kernel(*refs, sb, steps, depth, conv_rows):
    x_ref, p_ref, sp_ref, sc_ref, sf_ref = refs[0:5]
    w = dict(zip(WEIGHT_NAMES, refs[5:5 + N_W]))
    y_ref, npool_ref, nconv_ref, nffn_ref = refs[5 + N_W:9 + N_W]
    (cs_ref,) = refs[9 + N_W:]
    layer = pl.program_id(0)
    seqs = pl.ds(pl.multiple_of(pl.program_id(1) * sb, sb), sb)
    m = sb * steps

    @pl.when(layer == 0)
    def _():
        y_ref[seqs, :, :] = x_ref[...]

    x = jnp.concatenate([y_ref[seqs, t, :] for t in range(steps)], axis=0)
    a, glu = _mixer_in(x, w)

    def pool_ext(i, lo, hi):
        if i < POOL_HIST:
            return sp_ref[i, :, lo:hi]
        return a[(i - POOL_HIST) * sb:(i - POOL_HIST + 1) * sb, lo:hi]

    zs = []
    for g, win in enumerate(POOL_WINDOWS):
        lo = g * POOL_GROUP
        rows = []
        for t in range(steps):
            i = POOL_HIST + t
            tok = pool_ext(i, lo, lo + POOL_GROUP)
            s = tok
            for j in range(1, win):
                s = s + pool_ext(i - j, lo, lo + POOL_GROUP)
            cnt = float(min(PAST_LEN + t + 1, win))
            rows.append(s / cnt - tok)
        zs.append(jnp.concatenate(rows, axis=0))
    ya = _pool_project(zs, w)
    for h in range(POOL_HIST):
        npool_ref[h] = pool_ext(h + steps, 0, D_POOL)

    def conv_ext(i, r0):
        if i < CONV_HIST:
            return sc_ref[i, r0:r0 + conv_rows, :]
        return glu[(i - CONV_HIST) * sb + r0:(i - CONV_HIST) * sb + r0 + conv_rows, :]

    for t in range(steps):
        for r0 in range(0, sb, conv_rows):
            acc = jnp.broadcast_to(w["b_dw"][...], (conv_rows, D_CONV))
            for k in range(CONV_WIDTH):
                acc = acc + w["w_dw"][k:k + 1, :] * conv_ext(t + k, r0)
            cs_ref[t * sb + r0:t * sb + r0 + conv_rows, :] = _conv_post(acc, w)
    for h in range(CONV_HIST):
        i = h + steps
        if i < CONV_HIST:
            nconv_ref[h] = sc_ref[i]
        else:
            nconv_ref[h] = glu[(i - CONV_HIST) * sb:(i - CONV_HIST + 1) * sb, :]

    x = _merge(x, ya, cs_ref[...], w)

    def conv_gate(gate, c0, cf):
        h0 = sf_ref[:, c0:c0 + cf]
        h1 = sf_ref[:, D_FF + c0:D_FF + c0 + cf]
        prev1 = jnp.concatenate([h1, gate[0:m - sb, :]], axis=0)
        prev2 = jnp.concatenate([h0, h1, gate[0:m - 2 * sb, :]], axis=0)
        wd = w["w_ffn_dw"]
        return (wd[0:1, c0:c0 + cf] * prev2 + wd[1:2, c0:c0 + cf] * prev1
                + wd[2:3, c0:c0 + cf] * gate + w["b_ffn_dw"][:, c0:c0 + cf])

    def keep_gate(gate, c0, cf):
        for h in range(FFN_HIST):
            tt = steps - FFN_HIST + h
            nffn_ref[:, h * D_FF + c0:h * D_FF + c0 + cf] = gate[tt * sb:(tt + 1) * sb, :]

    x = _ffn(x, w, conv_gate, keep_gate)
    p = jnp.concatenate([p_ref[:, t, :] for t in range(steps)], axis=0)
    x = _ple(x, p, w)

    @pl.when(layer < depth - 1)
    def _():
        for t in range(steps):
            y_ref[seqs, t, :] = x[t * sb:(t + 1) * sb, :]

    @pl.when(layer == depth - 1)
    def _():
        y = _rms(x, w["final_norm"][...])
        for t in range(steps):
            y_ref[seqs, t, :] = y[t * sb:(t + 1) * sb, :]


def _sample_trunk(x, p, st_pool, st_conv, st_ffn, weights, sb=32, conv_rows=32):
    nseq, steps, _ = x.shape
    depth = p.shape[0]
    nblk = nseq // sb
    sp = jnp.swapaxes(st_pool, 1, 2)
    sc = jnp.swapaxes(st_conv, 1, 2)
    sf = st_ffn.reshape(depth, nseq, FFN_HIST * D_FF)

    def state_spec(hist, width):
        return pl.BlockSpec((None, hist, sb, width), lambda l, s: (l, 0, s, 0))

    ffn_spec = pl.BlockSpec((None, sb, FFN_HIST * D_FF), lambda l, s: (l, s, 0))
    in_specs = [
        pl.BlockSpec((sb, steps, D_MODEL), lambda l, s: (s, 0, 0)),
        pl.BlockSpec((None, sb, steps, D_PLE), lambda l, s: (l, s, 0, 0)),
        state_spec(POOL_HIST, D_POOL), state_spec(CONV_HIST, D_CONV), ffn_spec,
    ] + _weight_specs(weights, 2)(lambda l, s: l)
    out_shape = (
        jax.ShapeDtypeStruct(x.shape, F32),
        jax.ShapeDtypeStruct(sp.shape, F32),
        jax.ShapeDtypeStruct(sc.shape, F32),
        jax.ShapeDtypeStruct(sf.shape, F32),
    )
    out_specs = (
        pl.BlockSpec((nseq, steps, D_MODEL), lambda l, s: (0, 0, 0)),
        state_spec(POOL_HIST, D_POOL), state_spec(CONV_HIST, D_CONV), ffn_spec,
    )
    scratch = [
        pltpu.VMEM((sb * steps, D_CONV), BF16),
    ]
    y, npool, nconv, nffn = pl.pallas_call(
        functools.partial(_sample_kernel, sb=sb, steps=steps, depth=depth, conv_rows=conv_rows),
        grid=(depth, nblk),
        in_specs=in_specs,
        out_specs=out_specs,
        out_shape=out_shape,
        scratch_shapes=scratch,
        compiler_params=pltpu.CompilerParams(
            dimension_semantics=("arbitrary", "arbitrary"),
            vmem_limit_bytes=VMEM_LIMIT_BYTES),
        name="sample_trunk",
    )(x, p, sp, sc, sf, *[weights[n] for n in WEIGHT_NAMES])
    return (y, jnp.swapaxes(npool, 1, 2), jnp.swapaxes(nconv, 1, 2),
            nffn.reshape(depth, nseq, FFN_HIST, D_FF))


def kernel(x_prompt, x_sample, state_pool, state_conv, state_ffn, p_prompt, p_sample, g_mix, w_in, w_pool, pool_scale, w_dw, b_dw, ln_g, ln_b, w_pw, g_out_a, g_out_b, w_out, g_ffn, w_ffn_in, w_ffn_dw, b_ffn_dw, w_ffn_out, w_ple, g_ple, w_ple_gate, final_norm):
    depth = g_mix.shape[0]
    row = lambda v: v.reshape(depth, 1, v.shape[-1])
    weights = {
        "g_mix": row(g_mix), "w_in": w_in.astype(BF16), "w_pool": w_pool.astype(BF16),
        "pool_scale": row(pool_scale), "w_dw": w_dw, "b_dw": row(b_dw), "ln_g": row(ln_g),
        "ln_b": row(ln_b), "w_pw": w_pw.astype(BF16), "g_out_a": row(g_out_a),
        "g_out_b": row(g_out_b), "w_out": w_out.astype(BF16), "g_ffn": row(g_ffn),
        "w_ffn_in": w_ffn_in.astype(BF16), "w_ffn_dw": w_ffn_dw, "b_ffn_dw": row(b_ffn_dw),
        "w_ffn_out": w_ffn_out.astype(BF16), "w_ple": w_ple.astype(BF16), "g_ple": row(g_ple),
        "w_ple_gate": w_ple_gate.astype(BF16), "final_norm": final_norm.reshape(1, D_MODEL),
    }

    y_s, pool_s, conv_s, ffn_s = _sample_trunk(x_sample, p_sample, state_pool, state_conv,
                                               state_ffn, weights)

    nb, seq, _ = x_prompt.shape
    x = jnp.swapaxes(x_prompt, 0, 1).reshape(seq * nb, D_MODEL)
    p = jnp.swapaxes(p_prompt, 1, 2).reshape(depth, seq * nb, D_PLE)
    pools, convs, ffns = [], [], []
    for layer in range(depth):
        x, npool, nconv, nffn = _prompt_layer(x, p, weights, layer, final=layer == depth - 1, nb=nb)
        pools.append(npool.reshape(POOL_HIST, nb, D_POOL))
        convs.append(nconv.reshape(CONV_HIST, nb, D_CONV))
        ffns.append(nffn.reshape(FFN_HIST, nb, D_FF))
    y_p = jnp.swapaxes(x.reshape(seq, nb, D_MODEL), 0, 1)
    unstack = lambda parts: jnp.swapaxes(jnp.stack(parts), 1, 2)
    return (y_p, y_s, unstack(pools), unstack(convs), unstack(ffns), pool_s, conv_s, ffn_s)
```

```python
import functools

import jax
import jax.numpy as jnp
from jax import lax
from jax.experimental import pallas as pl
from jax.experimental.pallas import tpu as pltpu

D_MODEL = 1024
D_POOL = 512
D_CONV = 512
POOL_WINDOWS = (2, 4, 8, 16)
POOL_GROUP = 128
POOL_HIST = 15
CONV_WIDTH = 31
CONV_HIST = 30
D_FF = 2816
FFN_HIST = 2
D_PLE = 256
RMS_EPS = 1e-6
LN_EPS = 1e-5
PAST_LEN = 16384

SUBLANES = 8
VMEM_LIMIT_BYTES = 60 * 1024 * 1024
FFN_OUT_BLOCK = 256
CONV_RELEASE_NUM, CONV_RELEASE_DEN = 1, 1
LATE_PIECES_PER_EARLY = 2

POOL_PAD = 16
CONV_PAD = 32
FFN_CHUNKS = ((0, 512), (512, 512), (1024, 512), (1536, 512), (2048, 512), (2560, 256))
FFN_CHUNK_MAX = 512

BF16 = jnp.bfloat16
F32 = jnp.float32


def _dot(a, b):
    return jnp.dot(a, b, preferred_element_type=F32)


def _rms(x, g):
    return x * lax.rsqrt(jnp.mean(x * x, axis=-1, keepdims=True) + RMS_EPS) * g


def _layernorm(x, g, b):
    mu = jnp.mean(x, axis=-1, keepdims=True)
    xc = x - mu
    var = jnp.mean(xc * xc, axis=-1, keepdims=True)
    return xc * lax.rsqrt(var + LN_EPS) * g + b


def _silu(x):
    return x * jax.nn.sigmoid(x)


def _exact_zero(v):
    bits = pltpu.bitcast(v, jnp.uint32)
    half = jnp.uint32(16)
    zero_bits = lax.shift_right_logical(lax.shift_right_logical(bits, half), half)
    return pltpu.bitcast(zero_bits, F32)


def _mixer_in(x, w):
    n = _rms(x, w["g_mix"][...]).astype(BF16)
    a = _dot(n, w["w_in"][:, 0:D_POOL])
    u1 = _dot(n, w["w_in"][:, D_POOL:D_POOL + D_CONV])
    u2 = _dot(n, w["w_in"][:, D_POOL + D_CONV:D_POOL + 2 * D_CONV])
    return a, u1 * jax.nn.sigmoid(u2)


def _pool_project(zs, w):
    ys = [_dot(z.astype(BF16), w["w_pool"][g]) for g, z in enumerate(zs)]
    return jnp.concatenate(ys, axis=1) * w["pool_scale"][...]


def _conv_post(c, w):
    return _silu(_layernorm(c, w["ln_g"][...], w["ln_b"][...])).astype(BF16)


def _merge(x, ya, cs, w):
    yb = _dot(cs, w["w_pw"][...])
    m = jnp.concatenate([_rms(ya, w["g_out_a"][...]), _rms(yb, w["g_out_b"][...])], axis=1)
    return x + _dot(m.astype(BF16), w["w_out"][...])


def _ffn(x, w, conv_gate, keep_gate):
    n2 = _rms(x, w["g_ffn"][...]).astype(BF16)
    acc = x
    for c0, cf in FFN_CHUNKS:
        gate = _dot(n2, w["w_ffn_in"][:, c0:c0 + cf])
        up = _dot(n2, w["w_ffn_in"][:, D_FF + c0:D_FF + c0 + cf])
        gc = conv_gate(gate, c0, cf)
        keep_gate(gate, c0, cf)
        h = (_silu(gc) * up).astype(BF16)
        acc = acc + _dot(h, w["w_ffn_out"][c0:c0 + cf, :])
    return acc


def _ple(x, p, w):
    e = _rms(_dot(p.astype(BF16), w["w_ple"][...]), w["g_ple"][...])
    return x + jax.nn.sigmoid(_dot(x.astype(BF16), w["w_ple_gate"][...])) * e


WEIGHT_NAMES = ("g_mix", "w_in", "w_pool", "pool_scale", "w_dw", "b_dw", "ln_g", "ln_b", "w_pw",
                "g_out_a", "g_out_b", "w_out", "g_ffn", "w_ffn_in", "w_ffn_dw", "b_ffn_dw",
                "w_ffn_out", "w_ple", "g_ple", "w_ple_gate", "final_norm")
N_W = len(WEIGHT_NAMES)


def _weight_specs(weights, ngrid):
    def spec(arr, pick_layer):
        rest = (0,) * (arr.ndim - 1)
        if pick_layer is None:
            return pl.BlockSpec(arr.shape, lambda *g: (0,) + rest, pipeline_mode=pl.Buffered(1))
        return pl.BlockSpec((None,) + arr.shape[1:], lambda *g: (pick_layer(*g),) + rest,
                            pipeline_mode=pl.Buffered(1))
    return lambda pick_layer: [
        spec(weights[n], None if n == "final_norm" else pick_layer) for n in WEIGHT_NAMES]


def _prompt_layer_kernel(*refs, tt, nb, conv_rows, final):
    x_ref, p_ref = refs[0], refs[1]
    w = dict(zip(WEIGHT_NAMES, refs[2:2 + N_W]))
    y_ref, npool_ref, nconv_ref, nffn_ref = refs[2 + N_W:6 + N_W]
    a_ext, c_ext, cs_ref, g_ext, ffn_hist, xmix, w_rep, h_ref = refs[6 + N_W:]
    step = pl.program_id(0)
    tm = tt * nb
    pool0 = POOL_PAD * nb
    conv0 = CONV_PAD * nb
    ffn0 = FFN_HIST * nb

    @pl.when(step == 0)
    def _():
        a_ext[0:pool0, :] = jnp.zeros((pool0, D_POOL), F32)
        c_ext[0:conv0, :] = jnp.zeros((conv0, D_CONV), F32)
        ffn_hist[...] = jnp.zeros((ffn0, D_FF), F32)
        xmix[...] = jnp.zeros((tm, D_MODEL), F32)
        for k in range(CONV_WIDTH):
            w_rep[k * SUBLANES:(k + 1) * SUBLANES, :] = jnp.broadcast_to(
                w["w_dw"][k:k + 1, :], (SUBLANES, D_CONV))

    def conv_gate(gate, c0, cf):
        g_ext[0:ffn0, 0:cf] = ffn_hist[:, c0:c0 + cf]
        g_ext[ffn0:ffn0 + tm, 0:cf] = gate
        wd = w["w_ffn_dw"]
        return (wd[0:1, c0:c0 + cf] * g_ext[0:tm, 0:cf]
                + wd[1:2, c0:c0 + cf] * g_ext[nb:nb + tm, 0:cf]
                + wd[2:3, c0:c0 + cf] * gate + w["b_ffn_dw"][:, c0:c0 + cf])

    def keep_gate(gate, c0, cf):
        ffn_hist[:, c0:c0 + cf] = gate[tm - ffn0:tm, :]
        nffn_ref[:, c0:c0 + cf] = gate[tm - ffn0:tm, :]

    def late_stage():
        x1 = xmix[...]
        n2 = _rms(x1, w["g_ffn"][...]).astype(BF16)
        yield
        for c0, cf in FFN_CHUNKS:
            gate = _dot(n2, w["w_ffn_in"][:, c0:c0 + cf])
            yield
            up = _dot(n2, w["w_ffn_in"][:, D_FF + c0:D_FF + c0 + cf])
            yield
            gc = conv_gate(gate, c0, cf)
            keep_gate(gate, c0, cf)
            h = _silu(gc) * up
            h_ref[:, c0:c0 + cf] = h.astype(BF16)
            ffn_done.append(h[tm - SUBLANES:tm, 0:POOL_GROUP])
            yield
        cols = []
        for n0 in range(0, D_MODEL, FFN_OUT_BLOCK):
            blk = x1[:, n0:n0 + FFN_OUT_BLOCK] + _dot(h_ref[...], w["w_ffn_out"][:, n0:n0 + FFN_OUT_BLOCK])
            ffn_done.append(blk[tm - SUBLANES:tm, 0:POOL_GROUP])
            cols.append(blk)
            yield
        acc = jnp.concatenate(cols, axis=1)
        e = _dot(p_ref[...].astype(BF16), w["w_ple"][...])
        yield
        gate_p = _dot(acc.astype(BF16), w["w_ple_gate"][...])
        yield
        xl = acc + jax.nn.sigmoid(gate_p) * _rms(e, w["g_ple"][...])
        if final:
            xl = _rms(xl, w["final_norm"][...])
        y_ref[...] = xl
        yield

    def early_stage():
        x = x_ref[...]
        a, glu = _mixer_in(x, w)
        a_ext[pool0:pool0 + tm, :] = a
        c_ext[conv0:conv0 + tm, :] = glu
        yield
        pos = step * tt + lax.broadcasted_iota(jnp.int32, (tm, 1), 0) // nb
        zs = []
        for g, win in enumerate(POOL_WINDOWS):
            lo = g * POOL_GROUP
            s = a_ext[pool0 - (win - 1) * nb:pool0 + tm, lo:lo + POOL_GROUP]
            span = 1
            while span < win:
                s = s[span * nb:, :] + s[:-span * nb, :]
                span *= 2
            inv_cnt = 1.0 / jnp.minimum(pos + 1, win).astype(F32)
            zs.append(s * inv_cnt - a[:, lo:lo + POOL_GROUP])
            if g % 2 == 1:
                yield
        ya = _pool_project(zs, w)
        npool_ref[...] = a_ext[tm + pool0 - POOL_HIST * nb:tm + pool0, :]
        a_ext[0:pool0, :] = a_ext[tm:tm + pool0, :]
        base = conv0 - CONV_HIST * nb
        groups = conv_rows // SUBLANES
        prev = None
        for i, r0 in enumerate(range(0, tm, conv_rows)):
            bias = w["b_dw"][...]
            lag = i * CONV_RELEASE_NUM // CONV_RELEASE_DEN - 1
            if lag >= 0:
                bias = bias + _exact_zero(ffn_done[lag])[0:1, 0:1]
            if prev is not None:
                bias = bias + _exact_zero(prev)[0:1, 0:1]
            acc = jnp.broadcast_to(bias, (groups, SUBLANES, D_CONV))
            for k in range(CONV_WIDTH):
                lo = base + r0 + k * nb
                taps = w_rep[k * SUBLANES:(k + 1) * SUBLANES, :]
                rows = c_ext[lo:lo + conv_rows, :]
                acc = acc + taps[None] * rows.reshape(groups, SUBLANES, D_CONV)
            acc = acc.reshape(conv_rows, D_CONV)
            prev = acc[conv_rows - SUBLANES:conv_rows, 0:POOL_GROUP]
            cs_ref[r0:r0 + conv_rows, :] = _conv_post(acc, w)
            yield
        nconv_ref[...] = c_ext[tm + conv0 - CONV_HIST * nb:tm + conv0, :]
        c_ext[0:conv0, :] = c_ext[tm:tm + conv0, :]
        yb = _dot(cs_ref[...], w["w_pw"][...])
        yield
        m = jnp.concatenate([_rms(ya, w["g_out_a"][...]), _rms(yb, w["g_out_b"][...])], axis=1)
        xmix[...] = x + _dot(m.astype(BF16), w["w_out"][...])
        yield

    ffn_done = []
    late, early = late_stage(), early_stage()
    next(late)
    for _ in early:
        for _ in range(LATE_PIECES_PER_EARLY):
            next(late, None)
    for _ in late:
        pass


def _prompt_layer(x, p, weights, layer, final, nb, tt=32, conv_rows=32):
    rows = x.shape[0]
    assert nb % SUBLANES == 0 and rows % (tt * nb) == 0
    tm = tt * nb
    last = rows // tm - 1
    early = lambda t: jnp.minimum(t, last)
    late = lambda t: jnp.maximum(t - 1, 0)
    in_specs = [
        pl.BlockSpec((tm, D_MODEL), lambda t: (early(t), 0)),
        pl.BlockSpec((None, tm, D_PLE), lambda t: (layer, late(t), 0)),
    ] + _weight_specs(weights, 1)(lambda t: layer)
    out_shape = (
        jax.ShapeDtypeStruct((rows, D_MODEL), F32),
        jax.ShapeDtypeStruct((POOL_HIST * nb, D_POOL), F32),
        jax.ShapeDtypeStruct((CONV_HIST * nb, D_CONV), F32),
        jax.ShapeDtypeStruct((FFN_HIST * nb, D_FF), F32),
    )
    out_specs = (
        pl.BlockSpec((tm, D_MODEL), lambda t: (late(t), 0)),
        pl.BlockSpec((POOL_HIST * nb, D_POOL), lambda t: (0, 0)),
        pl.BlockSpec((CONV_HIST * nb, D_CONV), lambda t: (0, 0)),
        pl.BlockSpec((FFN_HIST * nb, D_FF), lambda t: (0, 0)),
    )
    scratch = [
        pltpu.VMEM((POOL_PAD * nb + tm, D_POOL), F32),
        pltpu.VMEM((CONV_PAD * nb + tm, D_CONV), F32),
        pltpu.VMEM((tm, D_CONV), BF16),
        pltpu.VMEM((FFN_HIST * nb + tm, FFN_CHUNK_MAX), F32),
        pltpu.VMEM((FFN_HIST * nb, D_FF), F32),
        pltpu.VMEM((tm, D_MODEL), F32),
        pltpu.VMEM((CONV_WIDTH * SUBLANES, D_CONV), F32),
        pltpu.VMEM((tm, D_FF), BF16),
    ]
    return pl.pallas_call(
        functools.partial(_prompt_layer_kernel, tt=tt, nb=nb, conv_rows=conv_rows, final=final),
        grid=(rows // tm + 1,),
        in_specs=in_specs,
        out_specs=out_specs,
        out_shape=out_shape,
        scratch_shapes=scratch,
        compiler_params=pltpu.CompilerParams(
            dimension_semantics=("arbitrary",),
            vmem_limit_bytes=VMEM_LIMIT_BYTES),
        name=f"prompt_layer{layer}",
    )(x, p, *[weights[n] for n in WEIGHT_NAMES])


def _sample_kernel(*refs, sb, steps, depth, conv_rows):
    x_ref, p_ref, sp_ref, sc_ref, sf_ref = refs[0:5]
    w = dict(zip(WEIGHT_NAMES, refs[5:5 + N_W]))
    y_ref, npool_ref, nconv_ref, nffn_ref = refs[5 + N_W:9 + N_W]
    (cs_ref,) = refs[9 + N_W:]
    layer = pl.program_id(0)
    seqs = pl.ds(pl.multiple_of(pl.program_id(1) * sb, sb), sb)
    m = sb * steps

    @pl.when(layer == 0)
    def _():
        y_ref[seqs, :, :] = x_ref[...]

    x = jnp.concatenate([y_ref[seqs, t, :] for t in range(steps)], axis=0)
    a, glu = _mixer_in(x, w)

    def pool_ext(i, lo, hi):
        if i < POOL_HIST:
            return sp_ref[i, :, lo:hi]
        return a[(i - POOL_HIST) * sb:(i - POOL_HIST + 1) * sb, lo:hi]

    zs = []
    for g, win in enumerate(POOL_WINDOWS):
        lo = g * POOL_GROUP
        rows = []
        for t in range(steps):
            i = POOL_HIST + t
            tok = pool_ext(i, lo, lo + POOL_GROUP)
            s = tok
            for j in range(1, win):
                s = s + pool_ext(i - j, lo, lo + POOL_GROUP)
            cnt = float(min(PAST_LEN + t + 1, win))
            rows.append(s / cnt - tok)
        zs.append(jnp.concatenate(rows, axis=0))
    ya = _pool_project(zs, w)
    for h in range(POOL_HIST):
        npool_ref[h] = pool_ext(h + steps, 0, D_POOL)

    def conv_ext(i, r0):
        if i < CONV_HIST:
            return sc_ref[i, r0:r0 + conv_rows, :]
        return glu[(i - CONV_HIST) * sb + r0:(i - CONV_HIST) * sb + r0 + conv_rows, :]

    for t in range(steps):
        for r0 in range(0, sb, conv_rows):
            acc = jnp.broadcast_to(w["b_dw"][...], (conv_rows, D_CONV))
            for k in range(CONV_WIDTH):
                acc = acc + w["w_dw"][k:k + 1, :] * conv_ext(t + k, r0)
            cs_ref[t * sb + r0:t * sb + r0 + conv_rows, :] = _conv_post(acc, w)
    for h in range(CONV_HIST):
        i = h + steps
        if i < CONV_HIST:
            nconv_ref[h] = sc_ref[i]
        else:
            nconv_ref[h] = glu[(i - CONV_HIST) * sb:(i - CONV_HIST + 1) * sb, :]

    x = _merge(x, ya, cs_ref[...], w)

    def conv_gate(gate, c0, cf):
        h0 = sf_ref[:, c0:c0 + cf]
        h1 = sf_ref[:, D_FF + c0:D_FF + c0 + cf]
        prev1 = jnp.concatenate([h1, gate[0:m - sb, :]], axis=0)
        prev2 = jnp.concatenate([h0, h1, gate[0:m - 2 * sb, :]], axis=0)
        wd = w["w_ffn_dw"]
        return (wd[0:1, c0:c0 + cf] * prev2 + wd[1:2, c0:c0 + cf] * prev1
                + wd[2:3, c0:c0 + cf] * gate + w["b_ffn_dw"][:, c0:c0 + cf])

    def keep_gate(gate, c0, cf):
        for h in range(FFN_HIST):
            tt = steps - FFN_HIST + h
            nffn_ref[:, h * D_FF + c0:h * D_FF + c0 + cf] = gate[tt * sb:(tt + 1) * sb, :]

    x = _ffn(x, w, conv_gate, keep_gate)
    p = jnp.concatenate([p_ref[:, t, :] for t in range(steps)], axis=0)
    x = _ple(x, p, w)

    @pl.when(layer < depth - 1)
    def _():
        for t in range(steps):
            y_ref[seqs, t, :] = x[t * sb:(t + 1) * sb, :]

    @pl.when(layer == depth - 1)
    def _():
        y = _rms(x, w["final_norm"][...])
        for t in range(steps):
            y_ref[seqs, t, :] = y[t * sb:(t + 1) * sb, :]


def _sample_trunk(x, p, st_pool, st_conv, st_ffn, weights, sb=32, conv_rows=32):
    nseq, steps, _ = x.shape
    depth = p.shape[0]
    nblk = nseq // sb
    sp = jnp.swapaxes(st_pool, 1, 2)
    sc = jnp.swapaxes(st_conv, 1, 2)
    sf = st_ffn.reshape(depth, nseq, FFN_HIST * D_FF)

    def state_spec(hist, width):
        return pl.BlockSpec((None, hist, sb, width), lambda l, s: (l, 0, s, 0))

    ffn_spec = pl.BlockSpec((None, sb, FFN_HIST * D_FF), lambda l, s: (l, s, 0))
    in_specs = [
        pl.BlockSpec((sb, steps, D_MODEL), lambda l, s: (s, 0, 0)),
        pl.BlockSpec((None, sb, steps, D_PLE), lambda l, s: (l, s, 0, 0)),
        state_spec(POOL_HIST, D_POOL), state_spec(CONV_HIST, D_CONV), ffn_spec,
    ] + _weight_specs(weights, 2)(lambda l, s: l)
    out_shape = (
        jax.ShapeDtypeStruct(x.shape, F32),
        jax.ShapeDtypeStruct(sp.shape, F32),
        jax.ShapeDtypeStruct(sc.shape, F32),
        jax.ShapeDtypeStruct(sf.shape, F32),
    )
    out_specs = (
        pl.BlockSpec((nseq, steps, D_MODEL), lambda l, s: (0, 0, 0)),
        state_spec(POOL_HIST, D_POOL), state_spec(CONV_HIST, D_CONV), ffn_spec,
    )
    scratch = [
        pltpu.VMEM((sb * steps, D_CONV), BF16),
    ]
    y, npool, nconv, nffn = pl.pallas_call(
        functools.partial(_sample_kernel, sb=sb, steps=steps, depth=depth, conv_rows=conv_rows),
        grid=(depth, nblk),
        in_specs=in_specs,
        out_specs=out_specs,
        out_shape=out_shape,
        scratch_shapes=scratch,
        compiler_params=pltpu.CompilerParams(
            dimension_semantics=("arbitrary", "arbitrary"),
            vmem_limit_bytes=VMEM_LIMIT_BYTES),
        name="sample_trunk",
    )(x, p, sp, sc, sf, *[weights[n] for n in WEIGHT_NAMES])
    return (y, jnp.swapaxes(npool, 1, 2), jnp.swapaxes(nconv, 1, 2),
            nffn.reshape(depth, nseq, FFN_HIST, D_FF))


def kernel(x_prompt, x_sample, state_pool, state_conv, state_ffn, p_prompt, p_sample, g_mix, w_in, w_pool, pool_scale, w_dw, b_dw, ln_g, ln_b, w_pw, g_out_a, g_out_b, w_out, g_ffn, w_ffn_in, w_ffn_dw, b_ffn_dw, w_ffn_out, w_ple, g_ple, w_ple_gate, final_norm):
    depth = g_mix.shape[0]
    row = lambda v: v.reshape(depth, 1, v.shape[-1])
    weights = {
        "g_mix": row(g_mix), "w_in": w_in.astype(BF16), "w_pool": w_pool.astype(BF16),
        "pool_scale": row(pool_scale), "w_dw": w_dw, "b_dw": row(b_dw), "ln_g": row(ln_g),
        "ln_b": row(ln_b), "w_pw": w_pw.astype(BF16), "g_out_a": row(g_out_a),
        "g_out_b": row(g_out_b), "w_out": w_out.astype(BF16), "g_ffn": row(g_ffn),
        "w_ffn_in": w_ffn_in.astype(BF16), "w_ffn_dw": w_ffn_dw, "b_ffn_dw": row(b_ffn_dw),
        "w_ffn_out": w_ffn_out.astype(BF16), "w_ple": w_ple.astype(BF16), "g_ple": row(g_ple),
        "w_ple_gate": w_ple_gate.astype(BF16), "final_norm": final_norm.reshape(1, D_MODEL),
    }

    y_s, pool_s, conv_s, ffn_s = _sample_trunk(x_sample, p_sample, state_pool, state_conv,
                                               state_ffn, weights)

    nb, seq, _ = x_prompt.shape
    x = jnp.swapaxes(x_prompt, 0, 1).reshape(seq * nb, D_MODEL)
    p = jnp.swapaxes(p_prompt, 1, 2).reshape(depth, seq * nb, D_PLE)
    pools, convs, ffns = [], [], []
    for layer in range(depth):
        x, npool, nconv, nffn = _prompt_layer(x, p, weights, layer, final=layer == depth - 1, nb=nb)
        pools.append(npool.reshape(POOL_HIST, nb, D_POOL))
        convs.append(nconv.reshape(CONV_HIST, nb, D_CONV))
        ffns.append(nffn.reshape(FFN_HIST, nb, D_FF))
    y_p = jnp.swapaxes(x.reshape(seq, nb, D_MODEL), 0, 1)
    unstack = lambda parts: jnp.swapaxes(jnp.stack(parts), 1, 2)
    return (y_p, y_s, unstack(pools), unstack(convs), unstack(ffns), pool_s, conv_s, ffn_s)
```

```python
import functools

import jax
import jax.numpy as jnp
from jax import lax
from jax.experimental import pallas as pl
from jax.experimental.pallas import tpu as pltpu

D_MODEL = 1024
D_POOL = 512
D_CONV = 512
POOL_WINDOWS = (2, 4, 8, 16)
POOL_GROUP = 128
POOL_HIST = 15
CONV_WIDTH = 31
CONV_HIST = 30
D_FF = 2816
FFN_HIST = 2
D_PLE = 256
RMS_EPS = 1e-6
LN_EPS = 1e-5
PAST_LEN = 16384

SUBLANES = 8
VMEM_LIMIT_BYTES = 60 * 1024 * 1024
FFN_OUT_BLOCK = 256
CONV_RELEASE_NUM, CONV_RELEASE_DEN = 1, 1
LATE_PIECES_PER_EARLY = 2

POOL_PAD = 16
CONV_PAD = 32
FFN_CHUNKS = ((0, 512), (512, 512), (1024, 512), (1536, 512), (2048, 512), (2560, 256))
FFN_CHUNK_MAX = 512

BF16 = jnp.bfloat16
F32 = jnp.float32


def _dot(a, b):
    return jnp.dot(a, b, preferred_element_type=F32)


def _rms(x, g):
    return x * lax.rsqrt(jnp.mean(x * x, axis=-1, keepdims=True) + RMS_EPS) * g


def _layernorm(x, g, b):
    mu = jnp.mean(x, axis=-1, keepdims=True)
    xc = x - mu
    var = jnp.mean(xc * xc, axis=-1, keepdims=True)
    return xc * lax.rsqrt(var + LN_EPS) * g + b


def _silu(x):
    return x * jax.nn.sigmoid(x)


def _exact_zero(v):
    bits = pltpu.bitcast(v, jnp.uint32)
    half = jnp.uint32(16)
    zero_bits = lax.shift_right_logical(lax.shift_right_logical(bits, half), half)
    return pltpu.bitcast(zero_bits, F32)


def _mixer_in(x, w):
    n = _rms(x, w["g_mix"][...]).astype(BF16)
    a = _dot(n, w["w_in"][:, 0:D_POOL])
    u1 = _dot(n, w["w_in"][:, D_POOL:D_POOL + D_CONV])
    u2 = _dot(n, w["w_in"][:, D_POOL + D_CONV:D_POOL + 2 * D_CONV])
    return a, u1 * jax.nn.sigmoid(u2)


def _pool_project(zs, w):
    ys = [_dot(z.astype(BF16), w["w_pool"][g]) for g, z in enumerate(zs)]
    return jnp.concatenate(ys, axis=1) * w["pool_scale"][...]


def _conv_post(c, w):
    return _silu(_layernorm(c, w["ln_g"][...], w["ln_b"][...])).astype(BF16)


def _merge(x, ya, cs, w):
    yb = _dot(cs, w["w_pw"][...])
    m = jnp.concatenate([_rms(ya, w["g_out_a"][...]), _rms(yb, w["g_out_b"][...])], axis=1)
    return x + _dot(m.astype(BF16), w["w_out"][...])


def _ffn(x, w, conv_gate, keep_gate, h_ref):
    n2 = _rms(x, w["g_ffn"][...]).astype(BF16)
    for c0, cf in FFN_CHUNKS:
        gate = _dot(n2, w["w_ffn_in"][:, c0:c0 + cf])
        up = _dot(n2, w["w_ffn_in"][:, D_FF + c0:D_FF + c0 + cf])
        gc = conv_gate(gate, c0, cf)
        keep_gate(gate, c0, cf)
        h_ref[:, c0:c0 + cf] = (_silu(gc) * up).astype(BF16)
    cols = [x[:, n0:n0 + FFN_OUT_BLOCK] + _dot(h_ref[...], w["w_ffn_out"][:, n0:n0 + FFN_OUT_BLOCK])
            for n0 in range(0, D_MODEL, FFN_OUT_BLOCK)]
    return jnp.concatenate(cols, axis=1)


def _ple(x, p, w):
    e = _rms(_dot(p.astype(BF16), w["w_ple"][...]), w["g_ple"][...])
    return x + jax.nn.sigmoid(_dot(x.astype(BF16), w["w_ple_gate"][...])) * e


WEIGHT_NAMES = ("g_mix", "w_in", "w_pool", "pool_scale", "w_dw", "b_dw", "ln_g", "ln_b", "w_pw",
                "g_out_a", "g_out_b", "w_out", "g_ffn", "w_ffn_in", "w_ffn_dw", "b_ffn_dw",
                "w_ffn_out", "w_ple", "g_ple", "w_ple_gate", "final_norm")
N_W = len(WEIGHT_NAMES)


def _weight_specs(weights, ngrid):
    def spec(arr, pick_layer):
        rest = (0,) * (arr.ndim - 1)
        if pick_layer is None:
            return pl.BlockSpec(arr.shape, lambda *g: (0,) + rest, pipeline_mode=pl.Buffered(1))
        return pl.BlockSpec((None,) + arr.shape[1:], lambda *g: (pick_layer(*g),) + rest,
                            pipeline_mode=pl.Buffered(1))
    return lambda pick_layer: [
        spec(weights[n], None if n == "final_norm" else pick_layer) for n in WEIGHT_NAMES]


def _prompt_layer_kernel(*refs, tt, nb, conv_rows, final):
    x_ref, p_ref = refs[0], refs[1]
    w = dict(zip(WEIGHT_NAMES, refs[2:2 + N_W]))
    y_ref, npool_ref, nconv_ref, nffn_ref = refs[2 + N_W:6 + N_W]
    a_ext, c_ext, cs_ref, g_ext, ffn_hist, xmix, w_rep, h_ref = refs[6 + N_W:]
    step = pl.program_id(0)
    tm = tt * nb
    pool0 = POOL_PAD * nb
    conv0 = CONV_PAD * nb
    ffn0 = FFN_HIST * nb

    @pl.when(step == 0)
    def _():
        a_ext[0:pool0, :] = jnp.zeros((pool0, D_POOL), F32)
        c_ext[0:conv0, :] = jnp.zeros((conv0, D_CONV), F32)
        ffn_hist[...] = jnp.zeros((ffn0, D_FF), F32)
        xmix[...] = jnp.zeros((tm, D_MODEL), F32)
        for k in range(CONV_WIDTH):
            w_rep[k * SUBLANES:(k + 1) * SUBLANES, :] = jnp.broadcast_to(
                w["w_dw"][k:k + 1, :], (SUBLANES, D_CONV))

    def conv_gate(gate, c0, cf):
        g_ext[0:ffn0, 0:cf] = ffn_hist[:, c0:c0 + cf]
        g_ext[ffn0:ffn0 + tm, 0:cf] = gate
        wd = w["w_ffn_dw"]
        return (wd[0:1, c0:c0 + cf] * g_ext[0:tm, 0:cf]
                + wd[1:2, c0:c0 + cf] * g_ext[nb:nb + tm, 0:cf]
                + wd[2:3, c0:c0 + cf] * gate + w["b_ffn_dw"][:, c0:c0 + cf])

    def keep_gate(gate, c0, cf):
        ffn_hist[:, c0:c0 + cf] = gate[tm - ffn0:tm, :]
        nffn_ref[:, c0:c0 + cf] = gate[tm - ffn0:tm, :]

    def late_stage():
        x1 = xmix[...]
        n2 = _rms(x1, w["g_ffn"][...]).astype(BF16)
        yield
        for c0, cf in FFN_CHUNKS:
            gate = _dot(n2, w["w_ffn_in"][:, c0:c0 + cf])
            yield
            up = _dot(n2, w["w_ffn_in"][:, D_FF + c0:D_FF + c0 + cf])
            yield
            gc = conv_gate(gate, c0, cf)
            keep_gate(gate, c0, cf)
            h = _silu(gc) * up
            h_ref[:, c0:c0 + cf] = h.astype(BF16)
            ffn_done.append(h[tm - SUBLANES:tm, 0:POOL_GROUP])
            yield
        cols = []
        for n0 in range(0, D_MODEL, FFN_OUT_BLOCK):
            blk = x1[:, n0:n0 + FFN_OUT_BLOCK] + _dot(h_ref[...], w["w_ffn_out"][:, n0:n0 + FFN_OUT_BLOCK])
            ffn_done.append(blk[tm - SUBLANES:tm, 0:POOL_GROUP])
            cols.append(blk)
            yield
        acc = jnp.concatenate(cols, axis=1)
        e = _dot(p_ref[...].astype(BF16), w["w_ple"][...])
        yield
        gate_p = _dot(acc.astype(BF16), w["w_ple_gate"][...])
        yield
        xl = acc + jax.nn.sigmoid(gate_p) * _rms(e, w["g_ple"][...])
        if final:
            xl = _rms(xl, w["final_norm"][...])
        y_ref[...] = xl
        yield

    def early_stage():
        x = x_ref[...]
        a, glu = _mixer_in(x, w)
        a_ext[pool0:pool0 + tm, :] = a
        c_ext[conv0:conv0 + tm, :] = glu
        yield
        pos = step * tt + lax.broadcasted_iota(jnp.int32, (tm, 1), 0) // nb
        zs = []
        for g, win in enumerate(POOL_WINDOWS):
            lo = g * POOL_GROUP
            s = a_ext[pool0 - (win - 1) * nb:pool0 + tm, lo:lo + POOL_GROUP]
            span = 1
            while span < win:
                s = s[span * nb:, :] + s[:-span * nb, :]
                span *= 2
            inv_cnt = 1.0 / jnp.minimum(pos + 1, win).astype(F32)
            zs.append(s * inv_cnt - a[:, lo:lo + POOL_GROUP])
            if g % 2 == 1:
                yield
        ya = _pool_project(zs, w)
        npool_ref[...] = a_ext[tm + pool0 - POOL_HIST * nb:tm + pool0, :]
        a_ext[0:pool0, :] = a_ext[tm:tm + pool0, :]
        base = conv0 - CONV_HIST * nb
        groups = conv_rows // SUBLANES
        prev = None
        for i, r0 in enumerate(range(0, tm, conv_rows)):
            bias = w["b_dw"][...]
            lag = i * CONV_RELEASE_NUM // CONV_RELEASE_DEN - 1
            if lag >= 0:
                bias = bias + _exact_zero(ffn_done[lag])[0:1, 0:1]
            if prev is not None:
                bias = bias + _exact_zero(prev)[0:1, 0:1]
            acc = jnp.broadcast_to(bias, (groups, SUBLANES, D_CONV))
            for k in range(CONV_WIDTH):
                lo = base + r0 + k * nb
                taps = w_rep[k * SUBLANES:(k + 1) * SUBLANES, :]
                rows = c_ext[lo:lo + conv_rows, :]
                acc = acc + taps[None] * rows.reshape(groups, SUBLANES, D_CONV)
            acc = acc.reshape(conv_rows, D_CONV)
            prev = acc[conv_rows - SUBLANES:conv_rows, 0:POOL_GROUP]
            cs_ref[r0:r0 + conv_rows, :] = _conv_post(acc, w)
            yield
        nconv_ref[...] = c_ext[tm + conv0 - CONV_HIST * nb:tm + conv0, :]
        c_ext[0:conv0, :] = c_ext[tm:tm + conv0, :]
        yb = _dot(cs_ref[...], w["w_pw"][...])
        yield
        m = jnp.concatenate([_rms(ya, w["g_out_a"][...]), _rms(yb, w["g_out_b"][...])], axis=1)
        xmix[...] = x + _dot(m.astype(BF16), w["w_out"][...])
        yield

    ffn_done = []
    late, early = late_stage(), early_stage()
    next(late)
    for _ in early:
        for _ in range(LATE_PIECES_PER_EARLY):
            next(late, None)
    for _ in late:
        pass


def _prompt_layer(x, p, weights, layer, final, nb, tt=32, conv_rows=32):
    rows = x.shape[0]
    assert nb % SUBLANES == 0 and rows % (tt * nb) == 0
    tm = tt * nb
    last = rows // tm - 1
    early = lambda t: jnp.minimum(t, last)
    late = lambda t: jnp.maximum(t - 1, 0)
    in_specs = [
        pl.BlockSpec((tm, D_MODEL), lambda t: (early(t), 0)),
        pl.BlockSpec((None, tm, D_PLE), lambda t: (layer, late(t), 0)),
    ] + _weight_specs(weights, 1)(lambda t: layer)
    out_shape = (
        jax.ShapeDtypeStruct((rows, D_MODEL), F32),
        jax.ShapeDtypeStruct((POOL_HIST * nb, D_POOL), F32),
        jax.ShapeDtypeStruct((CONV_HIST * nb, D_CONV), F32),
        jax.ShapeDtypeStruct((FFN_HIST * nb, D_FF), F32),
    )
    out_specs = (
        pl.BlockSpec((tm, D_MODEL), lambda t: (late(t), 0)),
        pl.BlockSpec((POOL_HIST * nb, D_POOL), lambda t: (0, 0)),
        pl.BlockSpec((CONV_HIST * nb, D_CONV), lambda t: (0, 0)),
        pl.BlockSpec((FFN_HIST * nb, D_FF), lambda t: (0, 0)),
    )
    scratch = [
        pltpu.VMEM((POOL_PAD * nb + tm, D_POOL), F32),
        pltpu.VMEM((CONV_PAD * nb + tm, D_CONV), F32),
        pltpu.VMEM((tm, D_CONV), BF16),
        pltpu.VMEM((FFN_HIST * nb + tm, FFN_CHUNK_MAX), F32),
        pltpu.VMEM((FFN_HIST * nb, D_FF), F32),
        pltpu.VMEM((tm, D_MODEL), F32),
        pltpu.VMEM((CONV_WIDTH * SUBLANES, D_CONV), F32),
        pltpu.VMEM((tm, D_FF), BF16),
    ]
    return pl.pallas_call(
        functools.partial(_prompt_layer_kernel, tt=tt, nb=nb, conv_rows=conv_rows, final=final),
        grid=(rows // tm + 1,),
        in_specs=in_specs,
        out_specs=out_specs,
        out_shape=out_shape,
        scratch_shapes=scratch,
        compiler_params=pltpu.CompilerParams(
            dimension_semantics=("arbitrary",),
            vmem_limit_bytes=VMEM_LIMIT_BYTES),
        name=f"prompt_layer{layer}",
    )(x, p, *[weights[n] for n in WEIGHT_NAMES])


def _sample_kernel(*refs, sb, steps, depth, conv_rows):
    x_ref, p_ref, sp_ref, sc_ref, sf_ref = refs[0:5]
    w = dict(zip(WEIGHT_NAMES, refs[5:5 + N_W]))
    y_ref, npool_ref, nconv_ref, nffn_ref = refs[5 + N_W:9 + N_W]
    cs_ref, h_ref = refs[9 + N_W:]
    layer = pl.program_id(0)
    seqs = pl.ds(pl.multiple_of(pl.program_id(1) * sb, sb), sb)
    m = sb * steps

    @pl.when(layer == 0)
    def _():
        y_ref[seqs, :, :] = x_ref[...]

    x = jnp.concatenate([y_ref[seqs, t, :] for t in range(steps)], axis=0)
    a, glu = _mixer_in(x, w)

    def pool_ext(i, lo, hi):
        if i < POOL_HIST:
            return sp_ref[i, :, lo:hi]
        return a[(i - POOL_HIST) * sb:(i - POOL_HIST + 1) * sb, lo:hi]

    zs = []
    for g, win in enumerate(POOL_WINDOWS):
        lo = g * POOL_GROUP
        rows = []
        for t in range(steps):
            i = POOL_HIST + t
            tok = pool_ext(i, lo, lo + POOL_GROUP)
            s = tok
            for j in range(1, win):
                s = s + pool_ext(i - j, lo, lo + POOL_GROUP)
            cnt = float(min(PAST_LEN + t + 1, win))
            rows.append(s / cnt - tok)
        zs.append(jnp.concatenate(rows, axis=0))
    ya = _pool_project(zs, w)
    for h in range(POOL_HIST):
        npool_ref[h] = pool_ext(h + steps, 0, D_POOL)

    def conv_ext(i, r0):
        if i < CONV_HIST:
            return sc_ref[i, r0:r0 + conv_rows, :]
        return glu[(i - CONV_HIST) * sb + r0:(i - CONV_HIST) * sb + r0 + conv_rows, :]

    for t in range(steps):
        for r0 in range(0, sb, conv_rows):
            acc = jnp.broadcast_to(w["b_dw"][...], (conv_rows, D_CONV))
            for k in range(CONV_WIDTH):
                acc = acc + w["w_dw"][k:k + 1, :] * conv_ext(t + k, r0)
            cs_ref[t * sb + r0:t * sb + r0 + conv_rows, :] = _conv_post(acc, w)
    for h in range(CONV_HIST):
        i = h + steps
        if i < CONV_HIST:
            nconv_ref[h] = sc_ref[i]
        else:
            nconv_ref[h] = glu[(i - CONV_HIST) * sb:(i - CONV_HIST + 1) * sb, :]

    x = _merge(x, ya, cs_ref[...], w)

    def conv_gate(gate, c0, cf):
        h0 = sf_ref[:, c0:c0 + cf]
        h1 = sf_ref[:, D_FF + c0:D_FF + c0 + cf]
        prev1 = jnp.concatenate([h1, gate[0:m - sb, :]], axis=0)
        prev2 = jnp.concatenate([h0, h1, gate[0:m - 2 * sb, :]], axis=0)
        wd = w["w_ffn_dw"]
        return (wd[0:1, c0:c0 + cf] * prev2 + wd[1:2, c0:c0 + cf] * prev1
                + wd[2:3, c0:c0 + cf] * gate + w["b_ffn_dw"][:, c0:c0 + cf])

    def keep_gate(gate, c0, cf):
        for h in range(FFN_HIST):
            tt = steps - FFN_HIST + h
            nffn_ref[:, h * D_FF + c0:h * D_FF + c0 + cf] = gate[tt * sb:(tt + 1) * sb, :]

    x = _ffn(x, w, conv_gate, keep_gate, h_ref)
    p = jnp.concatenate([p_ref[:, t, :] for t in range(steps)], axis=0)
    x = _ple(x, p, w)

    @pl.when(layer < depth - 1)
    def _():
        for t in range(steps):
            y_ref[seqs, t, :] = x[t * sb:(t + 1) * sb, :]

    @pl.when(layer == depth - 1)
    def _():
        y = _rms(x, w["final_norm"][...])
        for t in range(steps):
            y_ref[seqs, t, :] = y[t * sb:(t + 1) * sb, :]


def _sample_trunk(x, p, st_pool, st_conv, st_ffn, weights, sb=32, conv_rows=32):
    nseq, steps, _ = x.shape
    depth = p.shape[0]
    nblk = nseq // sb
    sp = jnp.swapaxes(st_pool, 1, 2)
    sc = jnp.swapaxes(st_conv, 1, 2)
    sf = st_ffn.reshape(depth, nseq, FFN_HIST * D_FF)

    def state_spec(hist, width):
        return pl.BlockSpec((None, hist, sb, width), lambda l, s: (l, 0, s, 0))

    ffn_spec = pl.BlockSpec((None, sb, FFN_HIST * D_FF), lambda l, s: (l, s, 0))
    in_specs = [
        pl.BlockSpec((sb, steps, D_MODEL), lambda l, s: (s, 0, 0)),
        pl.BlockSpec((None, sb, steps, D_PLE), lambda l, s: (l, s, 0, 0)),
        state_spec(POOL_HIST, D_POOL), state_spec(CONV_HIST, D_CONV), ffn_spec,
    ] + _weight_specs(weights, 2)(lambda l, s: l)
    out_shape = (
        jax.ShapeDtypeStruct(x.shape, F32),
        jax.ShapeDtypeStruct(sp.shape, F32),
        jax.ShapeDtypeStruct(sc.shape, F32),
        jax.ShapeDtypeStruct(sf.shape, F32),
    )
    out_specs = (
        pl.BlockSpec((nseq, steps, D_MODEL), lambda l, s: (0, 0, 0)),
        state_spec(POOL_HIST, D_POOL), state_spec(CONV_HIST, D_CONV), ffn_spec,
    )
    scratch = [
        pltpu.VMEM((sb * steps, D_CONV), BF16),
        pltpu.VMEM((sb * steps, D_FF), BF16),
    ]
    y, npool, nconv, nffn = pl.pallas_call(
        functools.partial(_sample_kernel, sb=sb, steps=steps, depth=depth, conv_rows=conv_rows),
        grid=(depth, nblk),
        in_specs=in_specs,
        out_specs=out_specs,
        out_shape=out_shape,
        scratch_shapes=scratch,
        compiler_params=pltpu.CompilerParams(
            dimension_semantics=("arbitrary", "arbitrary"),
            vmem_limit_bytes=VMEM_LIMIT_BYTES),
        name="sample_trunk",
    )(x, p, sp, sc, sf, *[weights[n] for n in WEIGHT_NAMES])
    return (y, jnp.swapaxes(npool, 1, 2), jnp.swapaxes(nconv, 1, 2),
            nffn.reshape(depth, nseq, FFN_HIST, D_FF))


def kernel(x_prompt, x_sample, state_pool, state_conv, state_ffn, p_prompt, p_sample, g_mix, w_in, w_pool, pool_scale, w_dw, b_dw, ln_g, ln_b, w_pw, g_out_a, g_out_b, w_out, g_ffn, w_ffn_in, w_ffn_dw, b_ffn_dw, w_ffn_out, w_ple, g_ple, w_ple_gate, final_norm):
    depth = g_mix.shape[0]
    row = lambda v: v.reshape(depth, 1, v.shape[-1])
    weights = {
        "g_mix": row(g_mix), "w_in": w_in.astype(BF16), "w_pool": w_pool.astype(BF16),
        "pool_scale": row(pool_scale), "w_dw": w_dw, "b_dw": row(b_dw), "ln_g": row(ln_g),
        "ln_b": row(ln_b), "w_pw": w_pw.astype(BF16), "g_out_a": row(g_out_a),
        "g_out_b": row(g_out_b), "w_out": w_out.astype(BF16), "g_ffn": row(g_ffn),
        "w_ffn_in": w_ffn_in.astype(BF16), "w_ffn_dw": w_ffn_dw, "b_ffn_dw": row(b_ffn_dw),
        "w_ffn_out": w_ffn_out.astype(BF16), "w_ple": w_ple.astype(BF16), "g_ple": row(g_ple),
        "w_ple_gate": w_ple_gate.astype(BF16), "final_norm": final_norm.reshape(1, D_MODEL),
    }

    y_s, pool_s, conv_s, ffn_s = _sample_trunk(x_sample, p_sample, state_pool, state_conv,
                                               state_ffn, weights)

    nb, seq, _ = x_prompt.shape
    x = jnp.swapaxes(x_prompt, 0, 1).reshape(seq * nb, D_MODEL)
    p = jnp.swapaxes(p_prompt, 1, 2).reshape(depth, seq * nb, D_PLE).astype(BF16)
    pools, convs, ffns = [], [], []
    for layer in range(depth):
        x, npool, nconv, nffn = _prompt_layer(x, p, weights, layer, final=layer == depth - 1, nb=nb)
        pools.append(npool.reshape(POOL_HIST, nb, D_POOL))
        convs.append(nconv.reshape(CONV_HIST, nb, D_CONV))
        ffns.append(nffn.reshape(FFN_HIST, nb, D_FF))
    y_p = jnp.swapaxes(x.reshape(seq, nb, D_MODEL), 0, 1)
    unstack = lambda parts: jnp.swapaxes(jnp.stack(parts), 1, 2)
    return (y_p, y_s, unstack(pools), unstack(convs), unstack(ffns), pool_s, conv_s, ffn_s)
```

```python
import functools

import jax
import jax.numpy as jnp
from jax import lax
from jax.experimental import pallas as pl
from jax.experimental.pallas import tpu as pltpu

D_MODEL = 1024
D_POOL = 512
D_CONV = 512
POOL_WINDOWS = (2, 4, 8, 16)
POOL_GROUP = 128
POOL_HIST = 15
CONV_WIDTH = 31
CONV_HIST = 30
D_FF = 2816
FFN_HIST = 2
D_PLE = 256
RMS_EPS = 1e-6
LN_EPS = 1e-5
PAST_LEN = 16384

SUBLANES = 8
VMEM_LIMIT_BYTES = 60 * 1024 * 1024
FFN_OUT_BLOCK = 256
CONV_RELEASE_NUM, CONV_RELEASE_DEN = 1, 1
LATE_PIECES_PER_EARLY = 2

POOL_PAD = 16
CONV_PAD = 32
FFN_CHUNKS = ((0, 512), (512, 512), (1024, 512), (1536, 512), (2048, 512), (2560, 256))
FFN_CHUNK_MAX = 512

BF16 = jnp.bfloat16
F32 = jnp.float32


def _dot(a, b):
    return jnp.dot(a, b, preferred_element_type=F32)


def _rms(x, g):
    return x * lax.rsqrt(jnp.mean(x * x, axis=-1, keepdims=True) + RMS_EPS) * g


def _layernorm(x, g, b):
    mu = jnp.mean(x, axis=-1, keepdims=True)
    xc = x - mu
    var = jnp.mean(xc * xc, axis=-1, keepdims=True)
    return xc * lax.rsqrt(var + LN_EPS) * g + b


def _silu(x):
    return x * jax.nn.sigmoid(x)


def _exact_zero(v):
    bits = pltpu.bitcast(v, jnp.uint32)
    half = jnp.uint32(16)
    zero_bits = lax.shift_right_logical(lax.shift_right_logical(bits, half), half)
    return pltpu.bitcast(zero_bits, F32)


def _mixer_in(x, w):
    n = _rms(x, w["g_mix"][...]).astype(BF16)
    a = _dot(n, w["w_in"][:, 0:D_POOL])
    u1 = _dot(n, w["w_in"][:, D_POOL:D_POOL + D_CONV])
    u2 = _dot(n, w["w_in"][:, D_POOL + D_CONV:D_POOL + 2 * D_CONV])
    return a, u1 * jax.nn.sigmoid(u2)


def _pool_project(zs, w):
    ys = [_dot(z.astype(BF16), w["w_pool"][g]) for g, z in enumerate(zs)]
    return jnp.concatenate(ys, axis=1) * w["pool_scale"][...]


def _conv_post(c, w):
    return _silu(_layernorm(c, w["ln_g"][...], w["ln_b"][...])).astype(BF16)


def _merge(x, ya, cs, w):
    yb = _dot(cs, w["w_pw"][...])
    m = jnp.concatenate([_rms(ya, w["g_out_a"][...]), _rms(yb, w["g_out_b"][...])], axis=1)
    return x + _dot(m.astype(BF16), w["w_out"][...])


def _ffn(x, w, conv_gate, keep_gate, h_ref):
    n2 = _rms(x, w["g_ffn"][...]).astype(BF16)
    for c0, cf in FFN_CHUNKS:
        gate = _dot(n2, w["w_ffn_in"][:, c0:c0 + cf])
        up = _dot(n2, w["w_ffn_in"][:, D_FF + c0:D_FF + c0 + cf])
        gc = conv_gate(gate, c0, cf)
        keep_gate(gate, c0, cf)
        h_ref[:, c0:c0 + cf] = (_silu(gc) * up).astype(BF16)
    cols = [x[:, n0:n0 + FFN_OUT_BLOCK] + _dot(h_ref[...], w["w_ffn_out"][:, n0:n0 + FFN_OUT_BLOCK])
            for n0 in range(0, D_MODEL, FFN_OUT_BLOCK)]
    return jnp.concatenate(cols, axis=1)


def _ple(x, p, w):
    e = _rms(_dot(p.astype(BF16), w["w_ple"][...]), w["g_ple"][...])
    return x + jax.nn.sigmoid(_dot(x.astype(BF16), w["w_ple_gate"][...])) * e


WEIGHT_NAMES = ("g_mix", "w_in", "w_pool", "pool_scale", "w_dw", "b_dw", "ln_g", "ln_b", "w_pw",
                "g_out_a", "g_out_b", "w_out", "g_ffn", "w_ffn_in", "w_ffn_dw", "b_ffn_dw",
                "w_ffn_out", "w_ple", "g_ple", "w_ple_gate", "final_norm")
N_W = len(WEIGHT_NAMES)


def _weight_specs(weights, ngrid):
    def spec(arr, pick_layer):
        rest = (0,) * (arr.ndim - 1)
        if pick_layer is None:
            return pl.BlockSpec(arr.shape, lambda *g: (0,) + rest, pipeline_mode=pl.Buffered(1))
        return pl.BlockSpec((None,) + arr.shape[1:], lambda *g: (pick_layer(*g),) + rest,
                            pipeline_mode=pl.Buffered(1))
    return lambda pick_layer: [
        spec(weights[n], None if n == "final_norm" else pick_layer) for n in WEIGHT_NAMES]


def _prompt_layer_kernel(*refs, tt, nb, conv_rows, final):
    x_ref, p_ref = refs[0], refs[1]
    w = dict(zip(WEIGHT_NAMES, refs[2:2 + N_W]))
    y_ref, npool_ref, nconv_ref, nffn_ref = refs[2 + N_W:6 + N_W]
    a_ext, c_ext, cs_ref, g_ext, ffn_hist, xmix, w_rep, h_ref = refs[6 + N_W:]
    step = pl.program_id(0)
    tm = tt * nb
    pool0 = POOL_PAD * nb
    conv0 = CONV_PAD * nb
    ffn0 = FFN_HIST * nb

    @pl.when(step == 0)
    def _():
        a_ext[0:pool0, :] = jnp.zeros((pool0, D_POOL), F32)
        c_ext[0:conv0, :] = jnp.zeros((conv0, D_CONV), F32)
        ffn_hist[...] = jnp.zeros((ffn0, D_FF), F32)
        xmix[...] = jnp.zeros((tm, D_MODEL), F32)
        for k in range(CONV_WIDTH):
            w_rep[k * SUBLANES:(k + 1) * SUBLANES, :] = jnp.broadcast_to(
                w["w_dw"][k:k + 1, :], (SUBLANES, D_CONV))

    def conv_gate(gate, c0, cf):
        g_ext[0:ffn0, 0:cf] = ffn_hist[:, c0:c0 + cf]
        g_ext[ffn0:ffn0 + tm, 0:cf] = gate
        wd = w["w_ffn_dw"]
        return (wd[0:1, c0:c0 + cf] * g_ext[0:tm, 0:cf]
                + wd[1:2, c0:c0 + cf] * g_ext[nb:nb + tm, 0:cf]
                + wd[2:3, c0:c0 + cf] * gate + w["b_ffn_dw"][:, c0:c0 + cf])

    def keep_gate(gate, c0, cf):
        ffn_hist[:, c0:c0 + cf] = gate[tm - ffn0:tm, :]
        nffn_ref[:, c0:c0 + cf] = gate[tm - ffn0:tm, :]

    def late_stage():
        x1 = xmix[...]
        n2 = _rms(x1, w["g_ffn"][...]).astype(BF16)
        yield
        for c0, cf in FFN_CHUNKS:
            gate = _dot(n2, w["w_ffn_in"][:, c0:c0 + cf])
            yield
            up = _dot(n2, w["w_ffn_in"][:, D_FF + c0:D_FF + c0 + cf])
            yield
            gc = conv_gate(gate, c0, cf)
            keep_gate(gate, c0, cf)
            h = _silu(gc) * up
            h_ref[:, c0:c0 + cf] = h.astype(BF16)
            ffn_done.append(h[tm - SUBLANES:tm, 0:POOL_GROUP])
            yield
        cols = []
        for n0 in range(0, D_MODEL, FFN_OUT_BLOCK):
            blk = x1[:, n0:n0 + FFN_OUT_BLOCK] + _dot(h_ref[...], w["w_ffn_out"][:, n0:n0 + FFN_OUT_BLOCK])
            ffn_done.append(blk[tm - SUBLANES:tm, 0:POOL_GROUP])
            cols.append(blk)
            yield
        acc = jnp.concatenate(cols, axis=1)
        e = _dot(p_ref[...].astype(BF16), w["w_ple"][...])
        yield
        gate_p = _dot(acc.astype(BF16), w["w_ple_gate"][...])
        yield
        xl = acc + jax.nn.sigmoid(gate_p) * _rms(e, w["g_ple"][...])
        if final:
            xl = _rms(xl, w["final_norm"][...])
        y_ref[...] = xl
        yield

    def early_stage():
        x = x_ref[...]
        a, glu = _mixer_in(x, w)
        a_ext[pool0:pool0 + tm, :] = a
        c_ext[conv0:conv0 + tm, :] = glu
        yield
        pos = step * tt + lax.broadcasted_iota(jnp.int32, (tm, 1), 0) // nb
        zs = []
        for g, win in enumerate(POOL_WINDOWS):
            lo = g * POOL_GROUP
            s = a_ext[pool0 - (win - 1) * nb:pool0 + tm, lo:lo + POOL_GROUP]
            span = 1
            while span < win:
                s = s[span * nb:, :] + s[:-span * nb, :]
                span *= 2
            inv_cnt = 1.0 / jnp.minimum(pos + 1, win).astype(F32)
            zs.append(s * inv_cnt - a[:, lo:lo + POOL_GROUP])
            if g % 2 == 1:
                yield
        ya = _pool_project(zs, w)
        npool_ref[...] = a_ext[tm + pool0 - POOL_HIST * nb:tm + pool0, :]
        a_ext[0:pool0, :] = a_ext[tm:tm + pool0, :]
        base = conv0 - CONV_HIST * nb
        groups = conv_rows // SUBLANES
        prev = None
        for i, r0 in enumerate(range(0, tm, conv_rows)):
            bias = w["b_dw"][...]
            lag = i * CONV_RELEASE_NUM // CONV_RELEASE_DEN - 1
            if lag >= 0:
                bias = bias + _exact_zero(ffn_done[lag])[0:1, 0:1]
            if prev is not None:
                bias = bias + _exact_zero(prev)[0:1, 0:1]
            acc = jnp.broadcast_to(bias, (groups, SUBLANES, D_CONV))
            for k in range(CONV_WIDTH):
                lo = base + r0 + k * nb
                taps = w_rep[k * SUBLANES:(k + 1) * SUBLANES, :]
                rows = c_ext[lo:lo + conv_rows, :]
                acc = acc + taps[None] * rows.reshape(groups, SUBLANES, D_CONV)
            acc = acc.reshape(conv_rows, D_CONV)
            prev = acc[conv_rows - SUBLANES:conv_rows, 0:POOL_GROUP]
            cs_ref[r0:r0 + conv_rows, :] = _conv_post(acc, w)
            yield
        nconv_ref[...] = c_ext[tm + conv0 - CONV_HIST * nb:tm + conv0, :]
        c_ext[0:conv0, :] = c_ext[tm:tm + conv0, :]
        yb = _dot(cs_ref[...], w["w_pw"][...])
        yield
        m = jnp.concatenate([_rms(ya, w["g_out_a"][...]), _rms(yb, w["g_out_b"][...])], axis=1)
        xmix[...] = x + _dot(m.astype(BF16), w["w_out"][...])
        yield

    ffn_done = []
    late, early = late_stage(), early_stage()
    next(late)
    for _ in early:
        for _ in range(LATE_PIECES_PER_EARLY):
            next(late, None)
    for _ in late:
        pass


def _prompt_layer(x, p, weights, layer, final, nb, tt=32, conv_rows=32):
    rows = x.shape[0]
    assert nb % SUBLANES == 0 and rows % (tt * nb) == 0
    tm = tt * nb
    last = rows // tm - 1
    early = lambda t: jnp.minimum(t, last)
    late = lambda t: jnp.maximum(t - 1, 0)
    in_specs = [
        pl.BlockSpec((tm, D_MODEL), lambda t: (early(t), 0)),
        pl.BlockSpec((None, tm, D_PLE), lambda t: (layer, late(t), 0)),
    ] + _weight_specs(weights, 1)(lambda t: layer)
    out_shape = (
        jax.ShapeDtypeStruct((rows, D_MODEL), F32),
        jax.ShapeDtypeStruct((POOL_HIST * nb, D_POOL), F32),
        jax.ShapeDtypeStruct((CONV_HIST * nb, D_CONV), F32),
        jax.ShapeDtypeStruct((FFN_HIST * nb, D_FF), F32),
    )
    out_specs = (
        pl.BlockSpec((tm, D_MODEL), lambda t: (late(t), 0)),
        pl.BlockSpec((POOL_HIST * nb, D_POOL), lambda t: (0, 0)),
        pl.BlockSpec((CONV_HIST * nb, D_CONV), lambda t: (0, 0)),
        pl.BlockSpec((FFN_HIST * nb, D_FF), lambda t: (0, 0)),
    )
    scratch = [
        pltpu.VMEM((POOL_PAD * nb + tm, D_POOL), F32),
        pltpu.VMEM((CONV_PAD * nb + tm, D_CONV), F32),
        pltpu.VMEM((tm, D_CONV), BF16),
        pltpu.VMEM((FFN_HIST * nb + tm, FFN_CHUNK_MAX), F32),
        pltpu.VMEM((FFN_HIST * nb, D_FF), F32),
        pltpu.VMEM((tm, D_MODEL), F32),
        pltpu.VMEM((CONV_WIDTH * SUBLANES, D_CONV), F32),
        pltpu.VMEM((tm, D_FF), BF16),
    ]
    return pl.pallas_call(
        functools.partial(_prompt_layer_kernel, tt=tt, nb=nb, conv_rows=conv_rows, final=final),
        grid=(rows // tm + 1,),
        in_specs=in_specs,
        out_specs=out_specs,
        out_shape=out_shape,
        scratch_shapes=scratch,
        compiler_params=pltpu.CompilerParams(
            dimension_semantics=("arbitrary",),
            vmem_limit_bytes=VMEM_LIMIT_BYTES),
        name=f"prompt_layer{layer}",
    )(x, p, *[weights[n] for n in WEIGHT_NAMES])


def _sample_kernel(*refs, sb, steps, depth, conv_rows):
    x_ref, p_ref, sp_ref, sc_ref, sf_ref = refs[0:5]
    w = dict(zip(WEIGHT_NAMES, refs[5:5 + N_W]))
    y_ref, npool_ref, nconv_ref, nffn_ref = refs[5 + N_W:9 + N_W]
    cs_ref, h_ref = refs[9 + N_W:]
    layer = pl.program_id(0)
    seqs = pl.ds(pl.multiple_of(pl.program_id(1) * sb, sb), sb)
    m = sb * steps

    @pl.when(layer == 0)
    def _():
        y_ref[seqs, :, :] = x_ref[...]

    x = jnp.concatenate([y_ref[seqs, t, :] for t in range(steps)], axis=0)
    a, glu = _mixer_in(x, w)

    def pool_ext(i, lo, hi):
        if i < POOL_HIST:
            return sp_ref[i, :, lo:hi]
        return a[(i - POOL_HIST) * sb:(i - POOL_HIST + 1) * sb, lo:hi]

    zs = []
    for g, win in enumerate(POOL_WINDOWS):
        lo = g * POOL_GROUP
        rows = []
        for t in range(steps):
            i = POOL_HIST + t
            tok = pool_ext(i, lo, lo + POOL_GROUP)
            s = tok
            for j in range(1, win):
                s = s + pool_ext(i - j, lo, lo + POOL_GROUP)
            cnt = float(min(PAST_LEN + t + 1, win))
            rows.append(s / cnt - tok)
        zs.append(jnp.concatenate(rows, axis=0))
    ya = _pool_project(zs, w)
    for h in range(POOL_HIST):
        npool_ref[h] = pool_ext(h + steps, 0, D_POOL)

    def conv_ext(i, r0):
        if i < CONV_HIST:
            return sc_ref[i, r0:r0 + conv_rows, :]
        return glu[(i - CONV_HIST) * sb + r0:(i - CONV_HIST) * sb + r0 + conv_rows, :]

    for t in range(steps):
        for r0 in range(0, sb, conv_rows):
            acc = jnp.broadcast_to(w["b_dw"][...], (conv_rows, D_CONV))
            for k in range(CONV_WIDTH):
                acc = acc + w["w_dw"][k:k + 1, :] * conv_ext(t + k, r0)
            cs_ref[t * sb + r0:t * sb + r0 + conv_rows, :] = _conv_post(acc, w)
    for h in range(CONV_HIST):
        i = h + steps
        if i < CONV_HIST:
            nconv_ref[h] = sc_ref[i]
        else:
            nconv_ref[h] = glu[(i - CONV_HIST) * sb:(i - CONV_HIST + 1) * sb, :]

    x = _merge(x, ya, cs_ref[...], w)

    def conv_gate(gate, c0, cf):
        h0 = sf_ref[:, 0, c0:c0 + cf]
        h1 = sf_ref[:, 1, c0:c0 + cf]
        prev1 = jnp.concatenate([h1, gate[0:m - sb, :]], axis=0)
        prev2 = jnp.concatenate([h0, h1, gate[0:m - 2 * sb, :]], axis=0)
        wd = w["w_ffn_dw"]
        return (wd[0:1, c0:c0 + cf] * prev2 + wd[1:2, c0:c0 + cf] * prev1
                + wd[2:3, c0:c0 + cf] * gate + w["b_ffn_dw"][:, c0:c0 + cf])

    def keep_gate(gate, c0, cf):
        for h in range(FFN_HIST):
            tt = steps - FFN_HIST + h
            nffn_ref[:, h, c0:c0 + cf] = gate[tt * sb:(tt + 1) * sb, :]

    x = _ffn(x, w, conv_gate, keep_gate, h_ref)
    p = jnp.concatenate([p_ref[:, t, :] for t in range(steps)], axis=0)
    x = _ple(x, p, w)

    @pl.when(layer < depth - 1)
    def _():
        for t in range(steps):
            y_ref[seqs, t, :] = x[t * sb:(t + 1) * sb, :]

    @pl.when(layer == depth - 1)
    def _():
        y = _rms(x, w["final_norm"][...])
        for t in range(steps):
            y_ref[seqs, t, :] = y[t * sb:(t + 1) * sb, :]


def _sample_trunk(x, p, st_pool, st_conv, st_ffn, weights, sb=32, conv_rows=32):
    nseq, steps, _ = x.shape
    depth = p.shape[0]
    nblk = nseq // sb
    sp = jnp.swapaxes(st_pool, 1, 2)
    sc = jnp.swapaxes(st_conv, 1, 2)

    def state_spec(hist, width):
        return pl.BlockSpec((None, hist, sb, width), lambda l, s: (l, 0, s, 0))

    ffn_spec = pl.BlockSpec((None, sb, FFN_HIST, D_FF), lambda l, s: (l, s, 0, 0))
    in_specs = [
        pl.BlockSpec((sb, steps, D_MODEL), lambda l, s: (s, 0, 0)),
        pl.BlockSpec((None, sb, steps, D_PLE), lambda l, s: (l, s, 0, 0)),
        state_spec(POOL_HIST, D_POOL), state_spec(CONV_HIST, D_CONV), ffn_spec,
    ] + _weight_specs(weights, 2)(lambda l, s: l)
    out_shape = (
        jax.ShapeDtypeStruct(x.shape, F32),
        jax.ShapeDtypeStruct(sp.shape, F32),
        jax.ShapeDtypeStruct(sc.shape, F32),
        jax.ShapeDtypeStruct(st_ffn.shape, F32),
    )
    out_specs = (
        pl.BlockSpec((nseq, steps, D_MODEL), lambda l, s: (0, 0, 0)),
        state_spec(POOL_HIST, D_POOL), state_spec(CONV_HIST, D_CONV), ffn_spec,
    )
    scratch = [
        pltpu.VMEM((sb * steps, D_CONV), BF16),
        pltpu.VMEM((sb * steps, D_FF), BF16),
    ]
    y, npool, nconv, nffn = pl.pallas_call(
        functools.partial(_sample_kernel, sb=sb, steps=steps, depth=depth, conv_rows=conv_rows),
        grid=(depth, nblk),
        in_specs=in_specs,
        out_specs=out_specs,
        out_shape=out_shape,
        scratch_shapes=scratch,
        compiler_params=pltpu.CompilerParams(
            dimension_semantics=("arbitrary", "arbitrary"),
            vmem_limit_bytes=VMEM_LIMIT_BYTES),
        name="sample_trunk",
    )(x, p, sp, sc, st_ffn, *[weights[n] for n in WEIGHT_NAMES])
    return y, jnp.swapaxes(npool, 1, 2), jnp.swapaxes(nconv, 1, 2), nffn


def kernel(x_prompt, x_sample, state_pool, state_conv, state_ffn, p_prompt, p_sample, g_mix, w_in, w_pool, pool_scale, w_dw, b_dw, ln_g, ln_b, w_pw, g_out_a, g_out_b, w_out, g_ffn, w_ffn_in, w_ffn_dw, b_ffn_dw, w_ffn_out, w_ple, g_ple, w_ple_gate, final_norm):
    depth = g_mix.shape[0]
    row = lambda v: v.reshape(depth, 1, v.shape[-1])
    weights = {
        "g_mix": row(g_mix), "w_in": w_in.astype(BF16), "w_pool": w_pool.astype(BF16),
        "pool_scale": row(pool_scale), "w_dw": w_dw, "b_dw": row(b_dw), "ln_g": row(ln_g),
        "ln_b": row(ln_b), "w_pw": w_pw.astype(BF16), "g_out_a": row(g_out_a),
        "g_out_b": row(g_out_b), "w_out": w_out.astype(BF16), "g_ffn": row(g_ffn),
        "w_ffn_in": w_ffn_in.astype(BF16), "w_ffn_dw": w_ffn_dw, "b_ffn_dw": row(b_ffn_dw),
        "w_ffn_out": w_ffn_out.astype(BF16), "w_ple": w_ple.astype(BF16), "g_ple": row(g_ple),
        "w_ple_gate": w_ple_gate.astype(BF16), "final_norm": final_norm.reshape(1, D_MODEL),
    }

    y_s, pool_s, conv_s, ffn_s = _sample_trunk(x_sample, p_sample, state_pool, state_conv,
                                               state_ffn, weights)

    nb, seq, _ = x_prompt.shape
    x = jnp.swapaxes(x_prompt, 0, 1).reshape(seq * nb, D_MODEL)
    p = jnp.swapaxes(p_prompt, 1, 2).reshape(depth, seq * nb, D_PLE).astype(BF16)
    pools, convs, ffns = [], [], []
    for layer in range(depth):
        x, npool, nconv, nffn = _prompt_layer(x, p, weights, layer, final=layer == depth - 1, nb=nb)
        pools.append(npool.reshape(POOL_HIST, nb, D_POOL))
        convs.append(nconv.reshape(CONV_HIST, nb, D_CONV))
        ffns.append(nffn.reshape(FFN_HIST, nb, D_FF))
    y_p = jnp.swapaxes(x.reshape(seq, nb, D_MODEL), 0, 1)
    unstack = lambda parts: jnp.swapaxes(jnp.stack(parts), 1, 2)
    return (y_p, y_s, unstack(pools), unstack(convs), unstack(ffns), pool_s, conv_s, ffn_s)
```

```python
import functools

import jax
import jax.numpy as jnp
from jax import lax
from jax.experimental import pallas as pl
from jax.experimental.pallas import tpu as pltpu

D_MODEL = 1024
D_POOL = 512
D_CONV = 512
POOL_WINDOWS = (2, 4, 8, 16)
POOL_GROUP = 128
POOL_HIST = 15
CONV_WIDTH = 31
CONV_HIST = 30
D_FF = 2816
FFN_HIST = 2
D_PLE = 256
RMS_EPS = 1e-6
LN_EPS = 1e-5
PAST_LEN = 16384

SUBLANES = 8
VMEM_LIMIT_BYTES = 60 * 1024 * 1024
FFN_OUT_BLOCK = 256
CONV_RELEASE_NUM, CONV_RELEASE_DEN = 1, 1
LATE_PIECES_PER_EARLY = 2

POOL_PAD = 16
CONV_PAD = 32
FFN_CHUNKS = ((0, 512), (512, 512), (1024, 512), (1536, 512), (2048, 512), (2560, 256))
FFN_CHUNK_MAX = 512

BF16 = jnp.bfloat16
F32 = jnp.float32


def _dot(a, b):
    return jnp.dot(a, b, preferred_element_type=F32)


def _rms(x, g):
    return x * lax.rsqrt(jnp.mean(x * x, axis=-1, keepdims=True) + RMS_EPS) * g


def _layernorm(x, g, b):
    mu = jnp.mean(x, axis=-1, keepdims=True)
    xc = x - mu
    var = jnp.mean(xc * xc, axis=-1, keepdims=True)
    return xc * lax.rsqrt(var + LN_EPS) * g + b


def _silu(x):
    return x * jax.nn.sigmoid(x)


def _exact_zero(v):
    bits = pltpu.bitcast(v, jnp.uint32)
    half = jnp.uint32(16)
    zero_bits = lax.shift_right_logical(lax.shift_right_logical(bits, half), half)
    return pltpu.bitcast(zero_bits, F32)


def _mixer_in(x, w):
    n = _rms(x, w["g_mix"][...]).astype(BF16)
    a = _dot(n, w["w_in"][:, 0:D_POOL])
    u1 = _dot(n, w["w_in"][:, D_POOL:D_POOL + D_CONV])
    u2 = _dot(n, w["w_in"][:, D_POOL + D_CONV:D_POOL + 2 * D_CONV])
    return a, u1 * jax.nn.sigmoid(u2)


def _pool_project(zs, w):
    ys = [_dot(z.astype(BF16), w["w_pool"][g]) for g, z in enumerate(zs)]
    return jnp.concatenate(ys, axis=1) * w["pool_scale"][...]


def _conv_post(c, w):
    return _silu(_layernorm(c, w["ln_g"][...], w["ln_b"][...])).astype(BF16)


def _merge(x, ya, cs, w):
    yb = _dot(cs, w["w_pw"][...])
    m = jnp.concatenate([_rms(ya, w["g_out_a"][...]), _rms(yb, w["g_out_b"][...])], axis=1)
    return x + _dot(m.astype(BF16), w["w_out"][...])


def _ffn(x, w, conv_gate, keep_gate, h_ref):
    n2 = _rms(x, w["g_ffn"][...]).astype(BF16)
    for c0, cf in FFN_CHUNKS:
        gate = _dot(n2, w["w_ffn_in"][:, c0:c0 + cf])
        up = _dot(n2, w["w_ffn_in"][:, D_FF + c0:D_FF + c0 + cf])
        gc = conv_gate(gate, c0, cf)
        keep_gate(gate, c0, cf)
        h_ref[:, c0:c0 + cf] = (_silu(gc) * up).astype(BF16)
    cols = [x[:, n0:n0 + FFN_OUT_BLOCK] + _dot(h_ref[...], w["w_ffn_out"][:, n0:n0 + FFN_OUT_BLOCK])
            for n0 in range(0, D_MODEL, FFN_OUT_BLOCK)]
    return jnp.concatenate(cols, axis=1)


def _ple(x, p, w):
    e = _rms(_dot(p.astype(BF16), w["w_ple"][...]), w["g_ple"][...])
    return x + jax.nn.sigmoid(_dot(x.astype(BF16), w["w_ple_gate"][...])) * e


WEIGHT_NAMES = ("g_mix", "w_in", "w_pool", "pool_scale", "w_dw", "b_dw", "ln_g", "ln_b", "w_pw",
                "g_out_a", "g_out_b", "w_out", "g_ffn", "w_ffn_in", "w_ffn_dw", "b_ffn_dw",
                "w_ffn_out", "w_ple", "g_ple", "w_ple_gate", "final_norm")
N_W = len(WEIGHT_NAMES)


def _weight_specs(weights, ngrid):
    def spec(arr, pick_layer):
        rest = (0,) * (arr.ndim - 1)
        if pick_layer is None:
            return pl.BlockSpec(arr.shape, lambda *g: (0,) + rest, pipeline_mode=pl.Buffered(1))
        return pl.BlockSpec((None,) + arr.shape[1:], lambda *g: (pick_layer(*g),) + rest,
                            pipeline_mode=pl.Buffered(1))
    return lambda pick_layer: [
        spec(weights[n], None if n == "final_norm" else pick_layer) for n in WEIGHT_NAMES]


def _prompt_layer_kernel(*refs, tt, nb, conv_rows, final):
    x_ref, p_ref = refs[0], refs[1]
    w = dict(zip(WEIGHT_NAMES, refs[2:2 + N_W]))
    y_ref, npool_ref, nconv_ref, nffn_ref = refs[2 + N_W:6 + N_W]
    a_ext, c_ext, cs_ref, g_ext, ffn_hist, xmix, w_rep, h_ref = refs[6 + N_W:]
    step = pl.program_id(0)
    tm = tt * nb
    pool0 = POOL_PAD * nb
    conv0 = CONV_PAD * nb
    ffn0 = FFN_HIST * nb

    @pl.when(step == 0)
    def _():
        a_ext[0:pool0, :] = jnp.zeros((pool0, D_POOL), F32)
        c_ext[0:conv0, :] = jnp.zeros((conv0, D_CONV), F32)
        ffn_hist[...] = jnp.zeros((ffn0, D_FF), F32)
        xmix[...] = jnp.zeros((tm, D_MODEL), F32)
        for k in range(CONV_WIDTH):
            w_rep[k * SUBLANES:(k + 1) * SUBLANES, :] = jnp.broadcast_to(
                w["w_dw"][k:k + 1, :], (SUBLANES, D_CONV))

    def conv_gate(gate, c0, cf):
        g_ext[0:ffn0, 0:cf] = ffn_hist[:, c0:c0 + cf]
        g_ext[ffn0:ffn0 + tm, 0:cf] = gate
        wd = w["w_ffn_dw"]
        return (wd[0:1, c0:c0 + cf] * g_ext[0:tm, 0:cf]
                + wd[1:2, c0:c0 + cf] * g_ext[nb:nb + tm, 0:cf]
                + wd[2:3, c0:c0 + cf] * gate + w["b_ffn_dw"][:, c0:c0 + cf])

    def keep_gate(gate, c0, cf):
        ffn_hist[:, c0:c0 + cf] = gate[tm - ffn0:tm, :]
        nffn_ref[:, c0:c0 + cf] = gate[tm - ffn0:tm, :]

    def late_stage():
        x1 = xmix[...]
        n2 = _rms(x1, w["g_ffn"][...]).astype(BF16)
        yield
        for c0, cf in FFN_CHUNKS:
            gate = _dot(n2, w["w_ffn_in"][:, c0:c0 + cf])
            yield
            up = _dot(n2, w["w_ffn_in"][:, D_FF + c0:D_FF + c0 + cf])
            yield
            gc = conv_gate(gate, c0, cf)
            keep_gate(gate, c0, cf)
            h = _silu(gc) * up
            h_ref[:, c0:c0 + cf] = h.astype(BF16)
            ffn_done.append(h[tm - SUBLANES:tm, 0:POOL_GROUP])
            yield
        cols = []
        for n0 in range(0, D_MODEL, FFN_OUT_BLOCK):
            blk = x1[:, n0:n0 + FFN_OUT_BLOCK] + _dot(h_ref[...], w["w_ffn_out"][:, n0:n0 + FFN_OUT_BLOCK])
            ffn_done.append(blk[tm - SUBLANES:tm, 0:POOL_GROUP])
            cols.append(blk)
            yield
        acc = jnp.concatenate(cols, axis=1)
        e = _dot(p_ref[...].astype(BF16), w["w_ple"][...])
        yield
        gate_p = _dot(acc.astype(BF16), w["w_ple_gate"][...])
        yield
        xl = acc + jax.nn.sigmoid(gate_p) * _rms(e, w["g_ple"][...])
        if final:
            xl = _rms(xl, w["final_norm"][...])
        y_ref[...] = xl
        yield

    def early_stage():
        x = x_ref[...]
        a, glu = _mixer_in(x, w)
        a_ext[pool0:pool0 + tm, :] = a
        c_ext[conv0:conv0 + tm, :] = glu
        yield
        pos = step * tt + lax.broadcasted_iota(jnp.int32, (tm, 1), 0) // nb
        zs = []
        for g, win in enumerate(POOL_WINDOWS):
            lo = g * POOL_GROUP
            s = a_ext[pool0 - (win - 1) * nb:pool0 + tm, lo:lo + POOL_GROUP]
            span = 1
            while span < win:
                s = s[span * nb:, :] + s[:-span * nb, :]
                span *= 2
            inv_cnt = 1.0 / jnp.minimum(pos + 1, win).astype(F32)
            zs.append(s * inv_cnt - a[:, lo:lo + POOL_GROUP])
            if g % 2 == 1:
                yield
        ya = _pool_project(zs, w)
        npool_ref[...] = a_ext[tm + pool0 - POOL_HIST * nb:tm + pool0, :]
        a_ext[0:pool0, :] = a_ext[tm:tm + pool0, :]
        base = conv0 - CONV_HIST * nb
        groups = conv_rows // SUBLANES
        prev = None
        for i, r0 in enumerate(range(0, tm, conv_rows)):
            bias = w["b_dw"][...]
            lag = i * CONV_RELEASE_NUM // CONV_RELEASE_DEN - 1
            if lag >= 0:
                bias = bias + _exact_zero(ffn_done[lag])[0:1, 0:1]
            if prev is not None:
                bias = bias + _exact_zero(prev)[0:1, 0:1]
            acc = jnp.broadcast_to(bias, (groups, SUBLANES, D_CONV))
            for k in range(CONV_WIDTH):
                lo = base + r0 + k * nb
                taps = w_rep[k * SUBLANES:(k + 1) * SUBLANES, :]
                rows = c_ext[lo:lo + conv_rows, :]
                acc = acc + taps[None] * rows.reshape(groups, SUBLANES, D_CONV)
            acc = acc.reshape(conv_rows, D_CONV)
            prev = acc[conv_rows - SUBLANES:conv_rows, 0:POOL_GROUP]
            cs_ref[r0:r0 + conv_rows, :] = _conv_post(acc, w)
            yield
        nconv_ref[...] = c_ext[tm + conv0 - CONV_HIST * nb:tm + conv0, :]
        c_ext[0:conv0, :] = c_ext[tm:tm + conv0, :]
        yb = _dot(cs_ref[...], w["w_pw"][...])
        yield
        m = jnp.concatenate([_rms(ya, w["g_out_a"][...]), _rms(yb, w["g_out_b"][...])], axis=1)
        xmix[...] = x + _dot(m.astype(BF16), w["w_out"][...])
        yield

    ffn_done = []
    late, early = late_stage(), early_stage()
    next(late)
    for _ in early:
        for _ in range(LATE_PIECES_PER_EARLY):
            next(late, None)
    for _ in late:
        pass


def _prompt_layer(x, p, weights, layer, final, nb, tt=32, conv_rows=32):
    rows = x.shape[0]
    assert nb % SUBLANES == 0 and rows % (tt * nb) == 0
    tm = tt * nb
    last = rows // tm - 1
    early = lambda t: jnp.minimum(t, last)
    late = lambda t: jnp.maximum(t - 1, 0)
    in_specs = [
        pl.BlockSpec((tm, D_MODEL), lambda t: (early(t), 0)),
        pl.BlockSpec((None, tm, D_PLE), lambda t: (layer, late(t), 0)),
    ] + _weight_specs(weights, 1)(lambda t: layer)
    out_shape = (
        jax.ShapeDtypeStruct((rows, D_MODEL), F32),
        jax.ShapeDtypeStruct((POOL_HIST * nb, D_POOL), F32),
        jax.ShapeDtypeStruct((CONV_HIST * nb, D_CONV), F32),
        jax.ShapeDtypeStruct((FFN_HIST * nb, D_FF), F32),
    )
    out_specs = (
        pl.BlockSpec((tm, D_MODEL), lambda t: (late(t), 0)),
        pl.BlockSpec((POOL_HIST * nb, D_POOL), lambda t: (0, 0)),
        pl.BlockSpec((CONV_HIST * nb, D_CONV), lambda t: (0, 0)),
        pl.BlockSpec((FFN_HIST * nb, D_FF), lambda t: (0, 0)),
    )
    scratch = [
        pltpu.VMEM((POOL_PAD * nb + tm, D_POOL), F32),
        pltpu.VMEM((CONV_PAD * nb + tm, D_CONV), F32),
        pltpu.VMEM((tm, D_CONV), BF16),
        pltpu.VMEM((FFN_HIST * nb + tm, FFN_CHUNK_MAX), F32),
        pltpu.VMEM((FFN_HIST * nb, D_FF), F32),
        pltpu.VMEM((tm, D_MODEL), F32),
        pltpu.VMEM((CONV_WIDTH * SUBLANES, D_CONV), F32),
        pltpu.VMEM((tm, D_FF), BF16),
    ]
    return pl.pallas_call(
        functools.partial(_prompt_layer_kernel, tt=tt, nb=nb, conv_rows=conv_rows, final=final),
        grid=(rows // tm + 1,),
        in_specs=in_specs,
        out_specs=out_specs,
        out_shape=out_shape,
        scratch_shapes=scratch,
        compiler_params=pltpu.CompilerParams(
            dimension_semantics=("arbitrary",),
            vmem_limit_bytes=VMEM_LIMIT_BYTES),
        name=f"prompt_layer{layer}",
    )(x, p, *[weights[n] for n in WEIGHT_NAMES])


def _sample_kernel(*refs, sb, steps, depth, conv_rows):
    x_ref, p_ref, sp_ref, sc_ref, sf_ref = refs[0:5]
    w = dict(zip(WEIGHT_NAMES, refs[5:5 + N_W]))
    y_ref, npool_ref, nconv_ref, nffn_ref = refs[5 + N_W:9 + N_W]
    cs_ref, h_ref = refs[9 + N_W:]
    layer = pl.program_id(0)
    seqs = pl.ds(pl.multiple_of(pl.program_id(1) * sb, sb), sb)
    m = sb * steps

    @pl.when(layer == 0)
    def _():
        y_ref[seqs, :, :] = x_ref[...]

    x = jnp.concatenate([y_ref[seqs, t, :] for t in range(steps)], axis=0)
    a, glu = _mixer_in(x, w)

    def pool_ext(i, lo, hi):
        if i < POOL_HIST:
            return sp_ref[i, :, lo:hi]
        return a[(i - POOL_HIST) * sb:(i - POOL_HIST + 1) * sb, lo:hi]

    zs = []
    for g, win in enumerate(POOL_WINDOWS):
        lo = g * POOL_GROUP
        rows = []
        for t in range(steps):
            i = POOL_HIST + t
            tok = pool_ext(i, lo, lo + POOL_GROUP)
            s = tok
            for j in range(1, win):
                s = s + pool_ext(i - j, lo, lo + POOL_GROUP)
            cnt = float(min(PAST_LEN + t + 1, win))
            rows.append(s / cnt - tok)
        zs.append(jnp.concatenate(rows, axis=0))
    ya = _pool_project(zs, w)
    for h in range(POOL_HIST):
        npool_ref[h] = pool_ext(h + steps, 0, D_POOL)

    def conv_ext(i, r0):
        if i < CONV_HIST:
            return sc_ref[i, r0:r0 + conv_rows, :]
        return glu[(i - CONV_HIST) * sb + r0:(i - CONV_HIST) * sb + r0 + conv_rows, :]

    for t in range(steps):
        for r0 in range(0, sb, conv_rows):
            acc = jnp.broadcast_to(w["b_dw"][...], (conv_rows, D_CONV))
            for k in range(CONV_WIDTH):
                acc = acc + w["w_dw"][k:k + 1, :] * conv_ext(t + k, r0)
            cs_ref[t * sb + r0:t * sb + r0 + conv_rows, :] = _conv_post(acc, w)
    for h in range(CONV_HIST):
        i = h + steps
        if i < CONV_HIST:
            nconv_ref[h] = sc_ref[i]
        else:
            nconv_ref[h] = glu[(i - CONV_HIST) * sb:(i - CONV_HIST + 1) * sb, :]

    x = _merge(x, ya, cs_ref[...], w)

    def conv_gate(gate, c0, cf):
        h0 = sf_ref[:, 0, c0:c0 + cf]
        h1 = sf_ref[:, 1, c0:c0 + cf]
        prev1 = jnp.concatenate([h1, gate[0:m - sb, :]], axis=0)
        prev2 = jnp.concatenate([h0, h1, gate[0:m - 2 * sb, :]], axis=0)
        wd = w["w_ffn_dw"]
        return (wd[0:1, c0:c0 + cf] * prev2 + wd[1:2, c0:c0 + cf] * prev1
                + wd[2:3, c0:c0 + cf] * gate + w["b_ffn_dw"][:, c0:c0 + cf])

    def keep_gate(gate, c0, cf):
        for h in range(FFN_HIST):
            tt = steps - FFN_HIST + h
            nffn_ref[:, h, c0:c0 + cf] = gate[tt * sb:(tt + 1) * sb, :]

    x = _ffn(x, w, conv_gate, keep_gate, h_ref)
    p = jnp.concatenate([p_ref[:, t, :] for t in range(steps)], axis=0)
    x = _ple(x, p, w)

    @pl.when(layer < depth - 1)
    def _():
        for t in range(steps):
            y_ref[seqs, t, :] = x[t * sb:(t + 1) * sb, :]

    @pl.when(layer == depth - 1)
    def _():
        y = _rms(x, w["final_norm"][...])
        for t in range(steps):
            y_ref[seqs, t, :] = y[t * sb:(t + 1) * sb, :]


def _sample_trunk(x, p, st_pool, st_conv, st_ffn, weights, sb=32, conv_rows=32):
    nseq, steps, _ = x.shape
    depth = p.shape[0]
    nblk = nseq // sb
    sp = jnp.swapaxes(st_pool, 1, 2)
    sc = jnp.swapaxes(st_conv, 1, 2)

    def state_spec(hist, width):
        return pl.BlockSpec((None, hist, sb, width), lambda l, s: (l, 0, s, 0))

    ffn_spec = pl.BlockSpec((None, sb, FFN_HIST, D_FF), lambda l, s: (l, s, 0, 0))
    in_specs = [
        pl.BlockSpec((sb, steps, D_MODEL), lambda l, s: (s, 0, 0)),
        pl.BlockSpec((None, sb, steps, D_PLE), lambda l, s: (l, s, 0, 0)),
        state_spec(POOL_HIST, D_POOL), state_spec(CONV_HIST, D_CONV), ffn_spec,
    ] + _weight_specs(weights, 2)(lambda l, s: l)
    out_shape = (
        jax.ShapeDtypeStruct(x.shape, F32),
        jax.ShapeDtypeStruct(sp.shape, F32),
        jax.ShapeDtypeStruct(sc.shape, F32),
        jax.ShapeDtypeStruct(st_ffn.shape, F32),
    )
    out_specs = (
        pl.BlockSpec((nseq, steps, D_MODEL), lambda l, s: (0, 0, 0)),
        state_spec(POOL_HIST, D_POOL), state_spec(CONV_HIST, D_CONV), ffn_spec,
    )
    scratch = [
        pltpu.VMEM((sb * steps, D_CONV), BF16),
        pltpu.VMEM((sb * steps, D_FF), BF16),
    ]
    y, npool, nconv, nffn = pl.pallas_call(
        functools.partial(_sample_kernel, sb=sb, steps=steps, depth=depth, conv_rows=conv_rows),
        grid=(depth, nblk),
        in_specs=in_specs,
        out_specs=out_specs,
        out_shape=out_shape,
        scratch_shapes=scratch,
        compiler_params=pltpu.CompilerParams(
            dimension_semantics=("arbitrary", "arbitrary"),
            vmem_limit_bytes=VMEM_LIMIT_BYTES),
        name="sample_trunk",
    )(x, p, sp, sc, st_ffn, *[weights[n] for n in WEIGHT_NAMES])
    return y, jnp.swapaxes(npool, 1, 2), jnp.swapaxes(nconv, 1, 2), nffn


def kernel(x_prompt, x_sample, state_pool, state_conv, state_ffn, p_prompt, p_sample, g_mix, w_in, w_pool, pool_scale, w_dw, b_dw, ln_g, ln_b, w_pw, g_out_a, g_out_b, w_out, g_ffn, w_ffn_in, w_ffn_dw, b_ffn_dw, w_ffn_out, w_ple, g_ple, w_ple_gate, final_norm):
    depth = g_mix.shape[0]
    row = lambda v: v.reshape(depth, 1, v.shape[-1])
    weights = {
        "g_mix": row(g_mix), "w_in": w_in.astype(BF16), "w_pool": w_pool.astype(BF16),
        "pool_scale": row(pool_scale), "w_dw": w_dw, "b_dw": row(b_dw), "ln_g": row(ln_g),
        "ln_b": row(ln_b), "w_pw": w_pw.astype(BF16), "g_out_a": row(g_out_a),
        "g_out_b": row(g_out_b), "w_out": w_out.astype(BF16), "g_ffn": row(g_ffn),
        "w_ffn_in": w_ffn_in.astype(BF16), "w_ffn_dw": w_ffn_dw, "b_ffn_dw": row(b_ffn_dw),
        "w_ffn_out": w_ffn_out.astype(BF16), "w_ple": w_ple.astype(BF16), "g_ple": row(g_ple),
        "w_ple_gate": w_ple_gate.astype(BF16), "final_norm": final_norm.reshape(1, D_MODEL),
    }

    y_s, pool_s, conv_s, ffn_s = _sample_trunk(x_sample, p_sample, state_pool, state_conv,
                                               state_ffn, weights)

    nb, seq, _ = x_prompt.shape
    x = jnp.swapaxes(x_prompt, 0, 1).reshape(seq * nb, D_MODEL)
    p = jnp.swapaxes(p_prompt, 1, 2).reshape(depth, seq * nb, D_PLE)
    pools, convs, ffns = [], [], []
    for layer in range(depth):
        x, npool, nconv, nffn = _prompt_layer(x, p, weights, layer, final=layer == depth - 1, nb=nb)
        pools.append(npool.reshape(POOL_HIST, nb, D_POOL))
        convs.append(nconv.reshape(CONV_HIST, nb, D_CONV))
        ffns.append(nffn.reshape(FFN_HIST, nb, D_FF))
    y_p = jnp.swapaxes(x.reshape(seq, nb, D_MODEL), 0, 1)
    unstack = lambda parts: jnp.swapaxes(jnp.stack(parts), 1, 2)
    return (y_p, y_s, unstack(pools), unstack(convs), unstack(ffns), pool_s, conv_s, ffn_s)
```

```python
import functools

import jax
import jax.numpy as jnp
from jax import lax
from jax.experimental import pallas as pl
from jax.experimental.pallas import tpu as pltpu

D_MODEL = 1024
D_POOL = 512
D_CONV = 512
POOL_WINDOWS = (2, 4, 8, 16)
POOL_GROUP = 128
POOL_HIST = 15
CONV_WIDTH = 31
CONV_HIST = 30
D_FF = 2816
FFN_HIST = 2
D_PLE = 256
RMS_EPS = 1e-6
LN_EPS = 1e-5
PAST_LEN = 16384

SUBLANES = 8
VMEM_LIMIT_BYTES = 60 * 1024 * 1024
FFN_OUT_BLOCK = 256
CONV_RELEASE_NUM, CONV_RELEASE_DEN = 1, 1
LATE_PIECES_PER_EARLY = 2

POOL_PAD = 16
CONV_PAD = 32
FFN_CHUNKS = ((0, 512), (512, 512), (1024, 512), (1536, 512), (2048, 512), (2560, 256))
FFN_CHUNK_MAX = 512

BF16 = jnp.bfloat16
F32 = jnp.float32


def _dot(a, b):
    return jnp.dot(a, b, preferred_element_type=F32)


def _rms(x, g):
    return x * lax.rsqrt(jnp.mean(x * x, axis=-1, keepdims=True) + RMS_EPS) * g


def _layernorm(x, g, b):
    mu = jnp.mean(x, axis=-1, keepdims=True)
    xc = x - mu
    var = jnp.mean(xc * xc, axis=-1, keepdims=True)
    return xc * lax.rsqrt(var + LN_EPS) * g + b


def _silu(x):
    return x * jax.nn.sigmoid(x)


def _exact_zero(v):
    bits = pltpu.bitcast(v, jnp.uint32)
    half = jnp.uint32(16)
    zero_bits = lax.shift_right_logical(lax.shift_right_logical(bits, half), half)
    return pltpu.bitcast(zero_bits, F32)


def _mixer_in(x, w):
    n = _rms(x, w["g_mix"][...]).astype(BF16)
    a = _dot(n, w["w_in"][:, 0:D_POOL])
    u1 = _dot(n, w["w_in"][:, D_POOL:D_POOL + D_CONV])
    u2 = _dot(n, w["w_in"][:, D_POOL + D_CONV:D_POOL + 2 * D_CONV])
    return a, u1 * jax.nn.sigmoid(u2)


def _pool_project(zs, w):
    ys = [_dot(z.astype(BF16), w["w_pool"][g]) for g, z in enumerate(zs)]
    return jnp.concatenate(ys, axis=1) * w["pool_scale"][...]


def _conv_post(c, w):
    return _silu(_layernorm(c, w["ln_g"][...], w["ln_b"][...])).astype(BF16)


def _merge(x, ya, cs, w):
    yb = _dot(cs, w["w_pw"][...])
    m = jnp.concatenate([_rms(ya, w["g_out_a"][...]), _rms(yb, w["g_out_b"][...])], axis=1)
    return x + _dot(m.astype(BF16), w["w_out"][...])


def _ffn(x, w, conv_gate, keep_gate, h_ref):
    n2 = _rms(x, w["g_ffn"][...]).astype(BF16)
    for c0, cf in FFN_CHUNKS:
        gate = _dot(n2, w["w_ffn_in"][:, c0:c0 + cf])
        up = _dot(n2, w["w_ffn_in"][:, D_FF + c0:D_FF + c0 + cf])
        gc = conv_gate(gate, c0, cf)
        keep_gate(gate, c0, cf)
        h_ref[:, c0:c0 + cf] = (_silu(gc) * up).astype(BF16)
    cols = [x[:, n0:n0 + FFN_OUT_BLOCK] + _dot(h_ref[...], w["w_ffn_out"][:, n0:n0 + FFN_OUT_BLOCK])
            for n0 in range(0, D_MODEL, FFN_OUT_BLOCK)]
    return jnp.concatenate(cols, axis=1)


def _ple(x, p, w):
    e = _rms(_dot(p.astype(BF16), w["w_ple"][...]), w["g_ple"][...])
    return x + jax.nn.sigmoid(_dot(x.astype(BF16), w["w_ple_gate"][...])) * e


WEIGHT_NAMES = ("g_mix", "w_in", "w_pool", "pool_scale", "w_dw", "b_dw", "ln_g", "ln_b", "w_pw",
                "g_out_a", "g_out_b", "w_out", "g_ffn", "w_ffn_in", "w_ffn_dw", "b_ffn_dw",
                "w_ffn_out", "w_ple", "g_ple", "w_ple_gate", "final_norm")
N_W = len(WEIGHT_NAMES)


def _weight_specs(weights, ngrid):
    def spec(arr, pick_layer):
        rest = (0,) * (arr.ndim - 1)
        if pick_layer is None:
            return pl.BlockSpec(arr.shape, lambda *g: (0,) + rest, pipeline_mode=pl.Buffered(1))
        return pl.BlockSpec((None,) + arr.shape[1:], lambda *g: (pick_layer(*g),) + rest,
                            pipeline_mode=pl.Buffered(1))
    return lambda pick_layer: [
        spec(weights[n], None if n == "final_norm" else pick_layer) for n in WEIGHT_NAMES]


def _time_major(ref, tt):
    return jnp.concatenate([ref[:, t, :] for t in range(tt)], axis=0)


def _prompt_layer_kernel(*refs, tt, nb, conv_rows, first, final):
    x_ref, p_ref = refs[0], refs[1]
    w = dict(zip(WEIGHT_NAMES, refs[2:2 + N_W]))
    y_ref, npool_ref, nconv_ref, nffn_ref = refs[2 + N_W:6 + N_W]
    a_ext, c_ext, cs_ref, g_ext, ffn_hist, xmix, w_rep, h_ref = refs[6 + N_W:]
    step = pl.program_id(0)
    tm = tt * nb
    pool0 = POOL_PAD * nb
    conv0 = CONV_PAD * nb
    ffn0 = FFN_HIST * nb

    @pl.when(step == 0)
    def _():
        a_ext[0:pool0, :] = jnp.zeros((pool0, D_POOL), F32)
        c_ext[0:conv0, :] = jnp.zeros((conv0, D_CONV), F32)
        ffn_hist[...] = jnp.zeros((ffn0, D_FF), F32)
        xmix[...] = jnp.zeros((tm, D_MODEL), F32)
        for k in range(CONV_WIDTH):
            w_rep[k * SUBLANES:(k + 1) * SUBLANES, :] = jnp.broadcast_to(
                w["w_dw"][k:k + 1, :], (SUBLANES, D_CONV))

    def conv_gate(gate, c0, cf):
        g_ext[0:ffn0, 0:cf] = ffn_hist[:, c0:c0 + cf]
        g_ext[ffn0:ffn0 + tm, 0:cf] = gate
        wd = w["w_ffn_dw"]
        return (wd[0:1, c0:c0 + cf] * g_ext[0:tm, 0:cf]
                + wd[1:2, c0:c0 + cf] * g_ext[nb:nb + tm, 0:cf]
                + wd[2:3, c0:c0 + cf] * gate + w["b_ffn_dw"][:, c0:c0 + cf])

    def keep_gate(gate, c0, cf):
        ffn_hist[:, c0:c0 + cf] = gate[tm - ffn0:tm, :]
        nffn_ref[:, c0:c0 + cf] = gate[tm - ffn0:tm, :]

    def late_stage():
        x1 = xmix[...]
        n2 = _rms(x1, w["g_ffn"][...]).astype(BF16)
        yield
        for c0, cf in FFN_CHUNKS:
            gate = _dot(n2, w["w_ffn_in"][:, c0:c0 + cf])
            yield
            up = _dot(n2, w["w_ffn_in"][:, D_FF + c0:D_FF + c0 + cf])
            yield
            gc = conv_gate(gate, c0, cf)
            keep_gate(gate, c0, cf)
            h = _silu(gc) * up
            h_ref[:, c0:c0 + cf] = h.astype(BF16)
            ffn_done.append(h[tm - SUBLANES:tm, 0:POOL_GROUP])
            yield
        cols = []
        for n0 in range(0, D_MODEL, FFN_OUT_BLOCK):
            blk = x1[:, n0:n0 + FFN_OUT_BLOCK] + _dot(h_ref[...], w["w_ffn_out"][:, n0:n0 + FFN_OUT_BLOCK])
            ffn_done.append(blk[tm - SUBLANES:tm, 0:POOL_GROUP])
            cols.append(blk)
            yield
        acc = jnp.concatenate(cols, axis=1)
        e = _dot(_time_major(p_ref, tt).astype(BF16), w["w_ple"][...])
        yield
        gate_p = _dot(acc.astype(BF16), w["w_ple_gate"][...])
        yield
        xl = acc + jax.nn.sigmoid(gate_p) * _rms(e, w["g_ple"][...])
        if final:
            xl = _rms(xl, w["final_norm"][...])
            for t in range(tt):
                y_ref[:, t, :] = xl[t * nb:(t + 1) * nb, :]
        else:
            y_ref[...] = xl
        yield

    def early_stage():
        x = _time_major(x_ref, tt) if first else x_ref[...]
        a, glu = _mixer_in(x, w)
        a_ext[pool0:pool0 + tm, :] = a
        c_ext[conv0:conv0 + tm, :] = glu
        yield
        pos = step * tt + lax.broadcasted_iota(jnp.int32, (tm, 1), 0) // nb
        zs = []
        for g, win in enumerate(POOL_WINDOWS):
            lo = g * POOL_GROUP
            s = a_ext[pool0 - (win - 1) * nb:pool0 + tm, lo:lo + POOL_GROUP]
            span = 1
            while span < win:
                s = s[span * nb:, :] + s[:-span * nb, :]
                span *= 2
            inv_cnt = 1.0 / jnp.minimum(pos + 1, win).astype(F32)
            zs.append(s * inv_cnt - a[:, lo:lo + POOL_GROUP])
            if g % 2 == 1:
                yield
        ya = _pool_project(zs, w)
        npool_ref[...] = a_ext[tm + pool0 - POOL_HIST * nb:tm + pool0, :]
        a_ext[0:pool0, :] = a_ext[tm:tm + pool0, :]
        base = conv0 - CONV_HIST * nb
        groups = conv_rows // SUBLANES
        prev = None
        for i, r0 in enumerate(range(0, tm, conv_rows)):
            bias = w["b_dw"][...]
            lag = i * CONV_RELEASE_NUM // CONV_RELEASE_DEN - 1
            if lag >= 0:
                bias = bias + _exact_zero(ffn_done[lag])[0:1, 0:1]
            if prev is not None:
                bias = bias + _exact_zero(prev)[0:1, 0:1]
            acc = jnp.broadcast_to(bias, (groups, SUBLANES, D_CONV))
            for k in range(CONV_WIDTH):
                lo = base + r0 + k * nb
                taps = w_rep[k * SUBLANES:(k + 1) * SUBLANES, :]
                rows = c_ext[lo:lo + conv_rows, :]
                acc = acc + taps[None] * rows.reshape(groups, SUBLANES, D_CONV)
            acc = acc.reshape(conv_rows, D_CONV)
            prev = acc[conv_rows - SUBLANES:conv_rows, 0:POOL_GROUP]
            cs_ref[r0:r0 + conv_rows, :] = _conv_post(acc, w)
            yield
        nconv_ref[...] = c_ext[tm + conv0 - CONV_HIST * nb:tm + conv0, :]
        c_ext[0:conv0, :] = c_ext[tm:tm + conv0, :]
        yb = _dot(cs_ref[...], w["w_pw"][...])
        yield
        m = jnp.concatenate([_rms(ya, w["g_out_a"][...]), _rms(yb, w["g_out_b"][...])], axis=1)
        xmix[...] = x + _dot(m.astype(BF16), w["w_out"][...])
        yield

    ffn_done = []
    late, early = late_stage(), early_stage()
    next(late)
    for _ in early:
        for _ in range(LATE_PIECES_PER_EARLY):
            next(late, None)
    for _ in late:
        pass


def _prompt_layer(x, p, weights, layer, first, final, tt=32, conv_rows=32):
    nb, seq = p.shape[1], p.shape[2]
    rows = seq * nb
    assert nb % SUBLANES == 0 and seq % tt == 0
    tm = tt * nb
    last = rows // tm - 1
    early = lambda t: jnp.minimum(t, last)
    late = lambda t: jnp.maximum(t - 1, 0)
    x_spec = (pl.BlockSpec((nb, tt, D_MODEL), lambda t: (0, early(t), 0)) if first
              else pl.BlockSpec((tm, D_MODEL), lambda t: (early(t), 0)))
    y_spec = (pl.BlockSpec((nb, tt, D_MODEL), lambda t: (0, late(t), 0)) if final
              else pl.BlockSpec((tm, D_MODEL), lambda t: (late(t), 0)))
    in_specs = [
        x_spec,
        pl.BlockSpec((None, nb, tt, D_PLE), lambda t: (layer, 0, late(t), 0)),
    ] + _weight_specs(weights, 1)(lambda t: layer)
    out_shape = (
        jax.ShapeDtypeStruct((nb, seq, D_MODEL) if final else (rows, D_MODEL), F32),
        jax.ShapeDtypeStruct((POOL_HIST * nb, D_POOL), F32),
        jax.ShapeDtypeStruct((CONV_HIST * nb, D_CONV), F32),
        jax.ShapeDtypeStruct((FFN_HIST * nb, D_FF), F32),
    )
    out_specs = (
        y_spec,
        pl.BlockSpec((POOL_HIST * nb, D_POOL), lambda t: (0, 0)),
        pl.BlockSpec((CONV_HIST * nb, D_CONV), lambda t: (0, 0)),
        pl.BlockSpec((FFN_HIST * nb, D_FF), lambda t: (0, 0)),
    )
    scratch = [
        pltpu.VMEM((POOL_PAD * nb + tm, D_POOL), F32),
        pltpu.VMEM((CONV_PAD * nb + tm, D_CONV), F32),
        pltpu.VMEM((tm, D_CONV), BF16),
        pltpu.VMEM((FFN_HIST * nb + tm, FFN_CHUNK_MAX), F32),
        pltpu.VMEM((FFN_HIST * nb, D_FF), F32),
        pltpu.VMEM((tm, D_MODEL), F32),
        pltpu.VMEM((CONV_WIDTH * SUBLANES, D_CONV), F32),
        pltpu.VMEM((tm, D_FF), BF16),
    ]
    return pl.pallas_call(
        functools.partial(_prompt_layer_kernel, tt=tt, nb=nb, conv_rows=conv_rows, first=first,
                          final=final),
        grid=(rows // tm + 1,),
        in_specs=in_specs,
        out_specs=out_specs,
        out_shape=out_shape,
        scratch_shapes=scratch,
        compiler_params=pltpu.CompilerParams(
            dimension_semantics=("arbitrary",),
            vmem_limit_bytes=VMEM_LIMIT_BYTES),
        name=f"prompt_layer{layer}",
    )(x, p, *[weights[n] for n in WEIGHT_NAMES])


def _sample_kernel(*refs, sb, steps, depth, conv_rows):
    x_ref, p_ref, sp_ref, sc_ref, sf_ref = refs[0:5]
    w = dict(zip(WEIGHT_NAMES, refs[5:5 + N_W]))
    y_ref, npool_ref, nconv_ref, nffn_ref = refs[5 + N_W:9 + N_W]
    cs_ref, h_ref = refs[9 + N_W:]
    layer = pl.program_id(0)
    seqs = pl.ds(pl.multiple_of(pl.program_id(1) * sb, sb), sb)
    m = sb * steps

    @pl.when(layer == 0)
    def _():
        y_ref[seqs, :, :] = x_ref[...]

    x = jnp.concatenate([y_ref[seqs, t, :] for t in range(steps)], axis=0)
    a, glu = _mixer_in(x, w)

    def pool_ext(i, lo, hi):
        if i < POOL_HIST:
            return sp_ref[i, :, lo:hi]
        return a[(i - POOL_HIST) * sb:(i - POOL_HIST + 1) * sb, lo:hi]

    zs = []
    for g, win in enumerate(POOL_WINDOWS):
        lo = g * POOL_GROUP
        rows = []
        for t in range(steps):
            i = POOL_HIST + t
            tok = pool_ext(i, lo, lo + POOL_GROUP)
            s = tok
            for j in range(1, win):
                s = s + pool_ext(i - j, lo, lo + POOL_GROUP)
            cnt = float(min(PAST_LEN + t + 1, win))
            rows.append(s / cnt - tok)
        zs.append(jnp.concatenate(rows, axis=0))
    ya = _pool_project(zs, w)
    for h in range(POOL_HIST):
        npool_ref[h] = pool_ext(h + steps, 0, D_POOL)

    def conv_ext(i, r0):
        if i < CONV_HIST:
            return sc_ref[i, r0:r0 + conv_rows, :]
        return glu[(i - CONV_HIST) * sb + r0:(i - CONV_HIST) * sb + r0 + conv_rows, :]

    for t in range(steps):
        for r0 in range(0, sb, conv_rows):
            acc = jnp.broadcast_to(w["b_dw"][...], (conv_rows, D_CONV))
            for k in range(CONV_WIDTH):
                acc = acc + w["w_dw"][k:k + 1, :] * conv_ext(t + k, r0)
            cs_ref[t * sb + r0:t * sb + r0 + conv_rows, :] = _conv_post(acc, w)
    for h in range(CONV_HIST):
        i = h + steps
        if i < CONV_HIST:
            nconv_ref[h] = sc_ref[i]
        else:
            nconv_ref[h] = glu[(i - CONV_HIST) * sb:(i - CONV_HIST + 1) * sb, :]

    x = _merge(x, ya, cs_ref[...], w)

    def conv_gate(gate, c0, cf):
        h0 = sf_ref[:, 0, c0:c0 + cf]
        h1 = sf_ref[:, 1, c0:c0 + cf]
        prev1 = jnp.concatenate([h1, gate[0:m - sb, :]], axis=0)
        prev2 = jnp.concatenate([h0, h1, gate[0:m - 2 * sb, :]], axis=0)
        wd = w["w_ffn_dw"]
        return (wd[0:1, c0:c0 + cf] * prev2 + wd[1:2, c0:c0 + cf] * prev1
                + wd[2:3, c0:c0 + cf] * gate + w["b_ffn_dw"][:, c0:c0 + cf])

    def keep_gate(gate, c0, cf):
        for h in range(FFN_HIST):
            tt = steps - FFN_HIST + h
            nffn_ref[:, h, c0:c0 + cf] = gate[tt * sb:(tt + 1) * sb, :]

    x = _ffn(x, w, conv_gate, keep_gate, h_ref)
    p = jnp.concatenate([p_ref[:, t, :] for t in range(steps)], axis=0)
    x = _ple(x, p, w)

    @pl.when(layer < depth - 1)
    def _():
        for t in range(steps):
            y_ref[seqs, t, :] = x[t * sb:(t + 1) * sb, :]

    @pl.when(layer == depth - 1)
    def _():
        y = _rms(x, w["final_norm"][...])
        for t in range(steps):
            y_ref[seqs, t, :] = y[t * sb:(t + 1) * sb, :]


def _sample_trunk(x, p, st_pool, st_conv, st_ffn, weights, sb=32, conv_rows=32):
    nseq, steps, _ = x.shape
    depth = p.shape[0]
    nblk = nseq // sb
    sp = jnp.swapaxes(st_pool, 1, 2)
    sc = jnp.swapaxes(st_conv, 1, 2)

    def state_spec(hist, width):
        return pl.BlockSpec((None, hist, sb, width), lambda l, s: (l, 0, s, 0))

    ffn_spec = pl.BlockSpec((None, sb, FFN_HIST, D_FF), lambda l, s: (l, s, 0, 0))
    in_specs = [
        pl.BlockSpec((sb, steps, D_MODEL), lambda l, s: (s, 0, 0)),
        pl.BlockSpec((None, sb, steps, D_PLE), lambda l, s: (l, s, 0, 0)),
        state_spec(POOL_HIST, D_POOL), state_spec(CONV_HIST, D_CONV), ffn_spec,
    ] + _weight_specs(weights, 2)(lambda l, s: l)
    out_shape = (
        jax.ShapeDtypeStruct(x.shape, F32),
        jax.ShapeDtypeStruct(sp.shape, F32),
        jax.ShapeDtypeStruct(sc.shape, F32),
        jax.ShapeDtypeStruct(st_ffn.shape, F32),
    )
    out_specs = (
        pl.BlockSpec((nseq, steps, D_MODEL), lambda l, s: (0, 0, 0)),
        state_spec(POOL_HIST, D_POOL), state_spec(CONV_HIST, D_CONV), ffn_spec,
    )
    scratch = [
        pltpu.VMEM((sb * steps, D_CONV), BF16),
        pltpu.VMEM((sb * steps, D_FF), BF16),
    ]
    y, npool, nconv, nffn = pl.pallas_call(
        functools.partial(_sample_kernel, sb=sb, steps=steps, depth=depth, conv_rows=conv_rows),
        grid=(depth, nblk),
        in_specs=in_specs,
        out_specs=out_specs,
        out_shape=out_shape,
        scratch_shapes=scratch,
        compiler_params=pltpu.CompilerParams(
            dimension_semantics=("arbitrary", "arbitrary"),
            vmem_limit_bytes=VMEM_LIMIT_BYTES),
        name="sample_trunk",
    )(x, p, sp, sc, st_ffn, *[weights[n] for n in WEIGHT_NAMES])
    return y, jnp.swapaxes(npool, 1, 2), jnp.swapaxes(nconv, 1, 2), nffn


def kernel(x_prompt, x_sample, state_pool, state_conv, state_ffn, p_prompt, p_sample, g_mix, w_in, w_pool, pool_scale, w_dw, b_dw, ln_g, ln_b, w_pw, g_out_a, g_out_b, w_out, g_ffn, w_ffn_in, w_ffn_dw, b_ffn_dw, w_ffn_out, w_ple, g_ple, w_ple_gate, final_norm):
    depth = g_mix.shape[0]
    row = lambda v: v.reshape(depth, 1, v.shape[-1])
    weights = {
        "g_mix": row(g_mix), "w_in": w_in.astype(BF16), "w_pool": w_pool.astype(BF16),
        "pool_scale": row(pool_scale), "w_dw": w_dw, "b_dw": row(b_dw), "ln_g": row(ln_g),
        "ln_b": row(ln_b), "w_pw": w_pw.astype(BF16), "g_out_a": row(g_out_a),
        "g_out_b": row(g_out_b), "w_out": w_out.astype(BF16), "g_ffn": row(g_ffn),
        "w_ffn_in": w_ffn_in.astype(BF16), "w_ffn_dw": w_ffn_dw, "b_ffn_dw": row(b_ffn_dw),
        "w_ffn_out": w_ffn_out.astype(BF16), "w_ple": w_ple.astype(BF16), "g_ple": row(g_ple),
        "w_ple_gate": w_ple_gate.astype(BF16), "final_norm": final_norm.reshape(1, D_MODEL),
    }

    y_s, pool_s, conv_s, ffn_s = _sample_trunk(x_sample, p_sample, state_pool, state_conv,
                                               state_ffn, weights)

    nb = x_prompt.shape[0]
    x = x_prompt
    pools, convs, ffns = [], [], []
    for layer in range(depth):
        x, npool, nconv, nffn = _prompt_layer(x, p_prompt, weights, layer, first=layer == 0,
                                              final=layer == depth - 1)
        pools.append(npool.reshape(POOL_HIST, nb, D_POOL))
        convs.append(nconv.reshape(CONV_HIST, nb, D_CONV))
        ffns.append(nffn.reshape(FFN_HIST, nb, D_FF))
    unstack = lambda parts: jnp.swapaxes(jnp.stack(parts), 1, 2)
    return (x, y_s, unstack(pools), unstack(convs), unstack(ffns), pool_s, conv_s, ffn_s)
```
